```python
import math
import jax, jax.numpy as jnp
from jax import lax
import numpy as np

D_MODEL = 2048
BATCH = 2
SEQ = 4096
DEPTH = 1

N_DIFF_HEADS = 8
DIFF_QK_DIM = 64
DIFF_V_DIM = 2 * DIFF_QK_DIM
N_DSA_HEADS = 8
DSA_HEAD_DIM = 128
N_IDX_HEADS = 16
IDX_DIM = 64
INDEX_TOPK = 256
D_FF = ((8 * D_MODEL // 3 + 255) // 256) * 256
ROPE_THETA = 500000.0
ROPE_FRACTION = 4
Q_BLOCK = 128
ADA_CHUNKS = 6
RMS_EPS = 1e-6

DIFF_Q = N_DIFF_HEADS * 2 * DIFF_QK_DIM
DIFF_K = N_DIFF_HEADS * 2 * DIFF_QK_DIM
DIFF_V = N_DIFF_HEADS * DIFF_V_DIM
DSA_Q = N_DSA_HEADS * DSA_HEAD_DIM
DSA_K = DSA_HEAD_DIM
DSA_V = DSA_HEAD_DIM
IDX_Q = N_IDX_HEADS * IDX_DIM
IDX_K = IDX_DIM
IDX_W = N_IDX_HEADS
PROJ_SIZES = (DIFF_Q, DIFF_K, DIFF_V, DSA_Q, DSA_K, DSA_V, IDX_Q, IDX_K, IDX_W)
PROJ_TOTAL = sum(PROJ_SIZES)
MIX_WIDTH = DIFF_V + N_DSA_HEADS * DSA_HEAD_DIM

kernel_name = "hybrid_diffattn_dsa_swiglu_adaln"


def rms_norm(x, g):
    xf = x.astype(jnp.float32)
    y = xf * lax.rsqrt(jnp.mean(xf * xf, axis=-1, keepdims=True) + RMS_EPS)
    return (y * g.astype(jnp.float32)).astype(x.dtype)


def rope_tables(positions, rot_dim):
    inv_freq = ROPE_THETA ** (-(jnp.arange(0, rot_dim, 2, dtype=jnp.float32) / rot_dim))
    ang = positions.astype(jnp.float32)[..., None] * inv_freq
    return jnp.cos(ang), jnp.sin(ang)


def partial_rope(x, positions):
    d = x.shape[-1]
    rot = d // ROPE_FRACTION
    cos, sin = rope_tables(positions, rot)
    cos, sin = cos[:, :, None, :], sin[:, :, None, :]
    xr = x[..., :rot].astype(jnp.float32)
    x1, x2 = xr[..., : rot // 2], xr[..., rot // 2:]
    rotated = jnp.concatenate([x1 * cos - x2 * sin, x2 * cos + x1 * sin], axis=-1)
    return jnp.concatenate([rotated.astype(x.dtype), x[..., rot:]], axis=-1)


def differential_attention(q1, q2, k1, k2, v, lam):
    B, S, H, dq = q1.shape
    scale = dq ** -0.5
    key_pos = jnp.arange(S)

    def block(i):
        start = i * Q_BLOCK
        qb1 = lax.dynamic_slice_in_dim(q1, start, Q_BLOCK, axis=1)
        qb2 = lax.dynamic_slice_in_dim(q2, start, Q_BLOCK, axis=1)
        q_pos = start + jnp.arange(Q_BLOCK)
        mask = key_pos[None, :] <= q_pos[:, None]
        s1 = jnp.einsum('bqhd,bkhd->bhqk', qb1, k1).astype(jnp.float32) * scale
        s2 = jnp.einsum('bqhd,bkhd->bhqk', qb2, k2).astype(jnp.float32) * scale
        p1 = jax.nn.softmax(jnp.where(mask, s1, -jnp.inf), axis=-1)
        p2 = jax.nn.softmax(jnp.where(mask, s2, -jnp.inf), axis=-1)
        attn = p1 - lam * p2
        return jnp.einsum('bhqk,bkhd->bqhd', attn.astype(v.dtype), v)

    out = lax.map(block, jnp.arange(S // Q_BLOCK))
    return out.transpose(1, 0, 2, 3, 4).reshape(B, S, H, v.shape[-1])


def indexed_sparse_attention(q, k, v, iq, ik, iw, k_top):
    B, S, H, D = q.shape
    scale = D ** -0.5
    key_pos = jnp.arange(S)

    def block(i):
        start = i * Q_BLOCK
        qb = lax.dynamic_slice_in_dim(q, start, Q_BLOCK, axis=1)
        iqb = lax.dynamic_slice_in_dim(iq, start, Q_BLOCK, axis=1)
        iwb = lax.dynamic_slice_in_dim(iw, start, Q_BLOCK, axis=1)
        q_pos = start + jnp.arange(Q_BLOCK)
        mask = key_pos[None, :] <= q_pos[:, None]
        logits = jnp.einsum('bqhd,bkd->bqhk', iqb, ik).astype(jnp.float32)
        score = jnp.einsum('bqhk,bqh->bqk', jax.nn.relu(logits), iwb.astype(jnp.float32))
        score = jnp.where(mask[None], score, -jnp.inf)
        _, sel = lax.top_k(score, k_top)
        valid = sel <= q_pos[None, :, None]
        k_sel = jax.vmap(lambda kb, ib: kb[ib])(k, sel)
        v_sel = jax.vmap(lambda vb, ib: vb[ib])(v, sel)
        s = jnp.einsum('bqhd,bqkd->bhqk', qb, k_sel).astype(jnp.float32) * scale
        s = jnp.where(valid[:, None], s, -jnp.inf)
        p = jax.nn.softmax(s, axis=-1)
        return jnp.einsum('bhqk,bqkd->bqhd', p.astype(v.dtype), v_sel)

    out = lax.map(block, jnp.arange(S // Q_BLOCK))
    return out.transpose(1, 0, 2, 3, 4).reshape(B, S, H, D)


def setup_inputs(seed: int = 0) -> dict:
    key = jax.random.key(seed)
    ks = jax.random.split(key, 20)
    f32 = jnp.float32

    def nrm(k, shape, scale):
        return jax.random.normal(k, shape, f32) * scale

    def gain(k, shape):
        return 1.0 + 0.05 * jax.random.normal(k, shape, f32)

    x = jax.random.normal(ks[0], (BATCH, SEQ, D_MODEL), f32)
    c = jax.random.normal(ks[1], (BATCH, D_MODEL), f32)
    offset = jax.random.randint(ks[2], (BATCH, 1), 0, 1024, dtype=jnp.int32)
    positions = (jnp.arange(SEQ, dtype=jnp.int32)[None, :] + offset).astype(jnp.int32)
    return {
        "x": x,
        "c": c,
        "positions": positions,
        "w_ada": nrm(ks[3], (DEPTH, D_MODEL, ADA_CHUNKS * D_MODEL), 0.5 * D_MODEL ** -0.5),
        "b_ada": nrm(ks[4], (DEPTH, ADA_CHUNKS * D_MODEL), 0.02),
        "g_attn_pre": gain(ks[5], (DEPTH, D_MODEL)),
        "g_attn_post": gain(ks[6], (DEPTH, D_MODEL)),
        "g_ffn_pre": gain(ks[7], (DEPTH, D_MODEL)),
        "g_ffn_post": gain(ks[8], (DEPTH, D_MODEL)),
        "w_in": nrm(ks[9], (DEPTH, D_MODEL, PROJ_TOTAL), D_MODEL ** -0.5),
        "lambda_q1": nrm(ks[10], (DEPTH, DIFF_QK_DIM), 0.1),
        "lambda_k1": nrm(ks[11], (DEPTH, DIFF_QK_DIM), 0.1),
        "lambda_q2": nrm(ks[12], (DEPTH, DIFF_QK_DIM), 0.1),
        "lambda_k2": nrm(ks[13], (DEPTH, DIFF_QK_DIM), 0.1),
        "g_diff_sub": gain(ks[14], (DEPTH, DIFF_V_DIM)),
        "w_o": nrm(ks[15], (DEPTH, MIX_WIDTH, D_MODEL), MIX_WIDTH ** -0.5),
        "w_gate": nrm(ks[16], (DEPTH, D_MODEL, D_FF), D_MODEL ** -0.5),
        "w_up": nrm(ks[17], (DEPTH, D_MODEL, D_FF), D_MODEL ** -0.5),
        "w_down": nrm(ks[18], (DEPTH, D_FF, D_MODEL), D_FF ** -0.5),
    }


def reference(x, c, positions, w_ada, b_ada, g_attn_pre, g_attn_post, g_ffn_pre,
              g_ffn_post, w_in, lambda_q1, lambda_k1, lambda_q2, lambda_k2,
              g_diff_sub, w_o, w_gate, w_up, w_down):
    B, S, _ = x.shape
    k_top = min(INDEX_TOPK, S // 4)
    offsets = [0]
    for sz in PROJ_SIZES:
        offsets.append(offsets[-1] + sz)
    idx_w_scale = (N_IDX_HEADS ** -0.5) * (IDX_DIM ** -0.5)

    for l in range(DEPTH):
        mod = jax.nn.silu(c) @ w_ada[l] + b_ada[l]
        shift_a, scale_a, gate_a, shift_f, scale_f, gate_f = jnp.split(mod, ADA_CHUNKS, axis=-1)

        h = rms_norm(x, g_attn_pre[l]) * (1.0 + scale_a[:, None]) + shift_a[:, None]
        proj = h @ w_in[l]
        parts = [proj[..., offsets[i]:offsets[i + 1]] for i in range(len(PROJ_SIZES))]
        d_q, d_k, d_v, s_q, s_k, s_v, i_q, i_k, i_w = parts

        d_q = d_q.reshape(B, S, N_DIFF_HEADS, 2, DIFF_QK_DIM)
        d_k = d_k.reshape(B, S, N_DIFF_HEADS, 2, DIFF_QK_DIM)
        q1 = partial_rope(d_q[..., 0, :], positions)
        q2 = partial_rope(d_q[..., 1, :], positions)
        k1 = partial_rope(d_k[..., 0, :], positions)
        k2 = partial_rope(d_k[..., 1, :], positions)
        d_v = d_v.reshape(B, S, N_DIFF_HEADS, DIFF_V_DIM)
        lam_init = 0.8 - 0.6 * math.exp(-0.3 * l)
        lam = (jnp.exp(jnp.sum(lambda_q1[l].astype(jnp.float32) * lambda_k1[l].astype(jnp.float32)))
               - jnp.exp(jnp.sum(lambda_q2[l].astype(jnp.float32) * lambda_k2[l].astype(jnp.float32)))
               + lam_init)
        diff_out = differential_attention(q1, q2, k1, k2, d_v, lam)
        diff_out = rms_norm(diff_out, g_diff_sub[l]) * (1.0 - lam_init)
        diff_out = diff_out.reshape(B, S, DIFF_V)

        s_q = partial_rope(s_q.reshape(B, S, N_DSA_HEADS, DSA_HEAD_DIM), positions)
        s_k = partial_rope(s_k.reshape(B, S, 1, DSA_HEAD_DIM), positions)[:, :, 0]
        i_q = partial_rope(i_q.reshape(B, S, N_IDX_HEADS, IDX_DIM), positions)
        i_k = partial_rope(i_k.reshape(B, S, 1, IDX_DIM), positions)[:, :, 0]
        i_w = i_w * idx_w_scale
        dsa_out = indexed_sparse_attention(s_q, s_k, s_v, i_q, i_k, i_w, k_top)
        dsa_out = dsa_out.reshape(B, S, N_DSA_HEADS * DSA_HEAD_DIM)

        mixed = jnp.concatenate([diff_out, dsa_out], axis=-1) @ w_o[l]
        x = x + gate_a[:, None] * rms_norm(mixed, g_attn_post[l])

        h = rms_norm(x, g_ffn_pre[l]) * (1.0 + scale_f[:, None]) + shift_f[:, None]
        ff = (jax.nn.silu(h @ w_gate[l]) * (h @ w_up[l])) @ w_down[l]
        x = x + gate_f[:, None] * rms_norm(ff, g_ffn_post[l])

    return x
```

```python
import functools
import math

import jax
import jax.numpy as jnp
from jax import lax
from jax.experimental import pallas as pl
from jax.experimental.pallas import tpu as pltpu

N_DIFF_HEADS = 8
DIFF_QK_DIM = 64
DIFF_V_DIM = 128
N_DSA_HEADS = 8
DSA_HEAD_DIM = 128
N_IDX_HEADS = 16
IDX_DIM = 64
INDEX_TOPK = 256
ROPE_THETA = 500000.0
ROPE_FRACTION = 4
ADA_CHUNKS = 6
RMS_EPS = 1e-6

DIFF_Q = N_DIFF_HEADS * 2 * DIFF_QK_DIM
DIFF_K = DIFF_Q
DIFF_V = N_DIFF_HEADS * DIFF_V_DIM
DSA_Q = N_DSA_HEADS * DSA_HEAD_DIM
DSA_K = DSA_HEAD_DIM
DSA_V = DSA_HEAD_DIM
IDX_Q = N_IDX_HEADS * IDX_DIM
IDX_K = IDX_DIM
IDX_W = N_IDX_HEADS
PROJ_SIZES = (DIFF_Q, DIFF_K, DIFF_V, DSA_Q, DSA_K, DSA_V, IDX_Q, IDX_K, IDX_W)

LANES = 128
MXU_WIDTH = 256
VMEM_LIMIT_BYTES = 56 * 1024 * 1024

F32 = jnp.float32
BF16 = jnp.bfloat16
INT_MIN = -(2 ** 31)
NEG_BIG = -1e30


def _cparams(sem):
    return pltpu.CompilerParams(dimension_semantics=sem, vmem_limit_bytes=VMEM_LIMIT_BYTES)


def _resident(shape, index_map):
    return pl.BlockSpec(shape, index_map, pipeline_mode=pl.Buffered(1))


def _rms(x, g):
    ms = jnp.mean(x * x, axis=-1, keepdims=True)
    return x * lax.rsqrt(ms + RMS_EPS) * g


def _ada_kernel(ct_ref, w_ref, b_ref, o_ref):
    k = pl.program_id(1)

    @pl.when(k == 0)
    def _():
        o_ref[...] = jnp.broadcast_to(b_ref[...], o_ref.shape)

    cv = ct_ref[...]
    s = cv * jax.nn.sigmoid(cv)
    w = w_ref[...]
    for b in range(o_ref.shape[0]):
        o_ref[b:b + 1, :] += jnp.sum(w * s[:, b:b + 1], axis=0, keepdims=True)


def _ada(c, w_ada, b_ada):
    bsz, d = c.shape
    n = w_ada.shape[1]
    tk, tn = 512, 2048
    return pl.pallas_call(
        _ada_kernel,
        grid=(n // tn, d // tk),
        in_specs=[
            pl.BlockSpec((tk, bsz), lambda j, k: (k, 0)),
            pl.BlockSpec((tk, tn), lambda j, k: (k, j)),
            pl.BlockSpec((1, tn), lambda j, k: (0, j)),
        ],
        out_specs=pl.BlockSpec((bsz, tn), lambda j, k: (0, j)),
        out_shape=jax.ShapeDtypeStruct((bsz, n), F32),
        compiler_params=_cparams(("parallel", "arbitrary")),
        name="ada",
    )(c.T, w_ada, b_ada.reshape(1, n))


def _rope_kernel(pos_ref, c64_ref, s64_ref, c128_ref, s128_ref):
    pos = pos_ref[...]
    lane = lax.broadcasted_iota(jnp.int32, (1, LANES), 1)
    for hd, c_ref, s_ref in ((64, c64_ref, s64_ref), (128, c128_ref, s128_ref)):
        rot = hd // ROPE_FRACTION
        half = rot // 2
        c = lane % hd
        k = (c % half).astype(F32)
        inv = jnp.power(jnp.float32(ROPE_THETA), -(k * 2.0 / rot))
        inv = jnp.where(c < rot, inv, 0.0)
        ang = pos * inv
        c_ref[...] = jnp.cos(ang)
        s_ref[...] = jnp.sin(ang) * jnp.where(c < half, -1.0, 1.0)


def _rope_tables(positions):
    bsz, s = positions.shape
    m = bsz * s
    posb = jnp.broadcast_to(positions.reshape(m, 1).astype(F32), (m, LANES))
    tm = min(1024, m)
    spec = pl.BlockSpec((tm, LANES), lambda i: (i, 0))
    shp = jax.ShapeDtypeStruct((m, LANES), F32)
    return pl.pallas_call(
        _rope_kernel,
        grid=(m // tm,),
        in_specs=[spec],
        out_specs=[spec] * 4,
        out_shape=[shp] * 4,
        compiler_params=_cparams(("parallel",)),
        name="rope",
    )(posb)


def _rope_apply(y, c_tab, s_tab, hd):
    n = y.shape[1]
    reps = n // LANES
    half = hd // ROPE_FRACTION // 2
    lane = lax.broadcasted_iota(jnp.int32, (1, n), 1) % hd
    cc = jnp.tile(c_tab, (1, reps)) if reps > 1 else c_tab
    ss = jnp.tile(s_tab, (1, reps)) if reps > 1 else s_tab
    ahead = pltpu.roll(y, n - half, axis=1)
    behind = pltpu.roll(y, half, axis=1)
    partner = jnp.where(lane < half, ahead, behind)
    return y * cc + partner * ss


def _proj_kernel(x_ref, g_ref, mod_ref, c64_ref, s64_ref, c128_ref, s128_ref,
                 wdq, wdk, wdv, wsq, wsk, wsv, wiq, wik, wiw,
                 odq, odk, odv, osq, osk, osv, oiq, oik, oiw):
    x = x_ref[...]
    h = _rms(x, g_ref[...]) * (1.0 + mod_ref[0, 1:2, :]) + mod_ref[0, 0:1, :]
    hb = h.astype(BF16)
    c64, s64 = c64_ref[...], s64_ref[...]
    c128, s128 = c128_ref[...], s128_ref[...]

    def part(w_ref, fn, store):
        n = w_ref.shape[1]
        for c0 in range(0, n, MXU_WIDTH):
            cw = min(MXU_WIDTH, n - c0)
            y = jnp.dot(hb, w_ref[:, c0:c0 + cw], preferred_element_type=F32)
            store(c0, cw, fn(y))

    def flat_store(o_ref):
        def st(c0, cw, y):
            o_ref[:, c0:c0 + cw] = y.astype(o_ref.dtype)
        return st

    def head_store(o_ref):
        def st(c0, cw, y):
            for j in range(cw // DSA_HEAD_DIM):
                hh = c0 // DSA_HEAD_DIM + j
                o_ref[hh] = y[:, j * DSA_HEAD_DIM:(j + 1) * DSA_HEAD_DIM].astype(o_ref.dtype)
        return st

    rope64 = lambda y: _rope_apply(y, c64, s64, 64)
    rope128 = lambda y: _rope_apply(y, c128, s128, 128)
    diff_scale = DIFF_QK_DIM ** -0.5
    dsa_scale = DSA_HEAD_DIM ** -0.5
    idx_w_scale = (N_IDX_HEADS ** -0.5) * (IDX_DIM ** -0.5)

    part(wdq, lambda y: rope64(y) * diff_scale, flat_store(odq))
    part(wdk, rope64, flat_store(odk))
    part(wdv, lambda y: y, flat_store(odv))
    part(wsq, lambda y: rope128(y) * dsa_scale, head_store(osq))
    part(wsk, rope128, flat_store(osk))
    part(wsv, lambda y: y, flat_store(osv))
    part(wiq, rope64, flat_store(oiq))
    part(wik, rope64, flat_store(oik))
    part(wiw, lambda y: y * idx_w_scale, flat_store(oiw))


def _proj(x2, g, mod3, tabs, wparts, seq):
    m, d = x2.shape
    tm = min(512, seq)
    tiles_per_batch = seq // tm
    row = lambda n: pl.BlockSpec((tm, n), lambda i: (i, 0))
    in_specs = [
        row(d),
        _resident((1, d), lambda i: (0, 0)),
        pl.BlockSpec((1, ADA_CHUNKS, d), lambda i: (i // tiles_per_batch, 0, 0)),
        row(LANES), row(LANES), row(LANES), row(LANES),
    ] + [_resident(w.shape, lambda i: (0, 0)) for w in wparts]
    out_specs = [
        row(DIFF_Q), row(DIFF_K), row(DIFF_V),
        pl.BlockSpec((N_DSA_HEADS, tm, DSA_HEAD_DIM), lambda i: (0, i, 0)),
        row(LANES), row(LANES), row(IDX_Q), row(LANES), row(LANES),
    ]
    sds = jax.ShapeDtypeStruct
    out_shape = [
        sds((m, DIFF_Q), BF16), sds((m, DIFF_K), BF16), sds((m, DIFF_V), BF16),
        sds((N_DSA_HEADS, m, DSA_HEAD_DIM), BF16),
        sds((m, LANES), BF16), sds((m, LANES), BF16), sds((m, IDX_Q), BF16),
        sds((m, LANES), BF16), sds((m, LANES), F32),
    ]
    return pl.pallas_call(
        _proj_kernel,
        grid=(m // tm,),
        in_specs=in_specs,
        out_specs=out_specs,
        out_shape=out_shape,
        compiler_params=_cparams(("parallel",)),
        name="proj",
    )(x2, g, mod3, *tabs, *wparts)


def _diff_kernel(q_ref, k_ref, v_ref, lq1, lk1, lq2, lk2, g_ref, o_ref,
                 m1, l1, a1, m2, l2, a2, *, tq, tk, lam_init):
    qi = pl.program_id(2)
    q = q_ref[...]
    lane = lax.broadcasted_iota(jnp.int32, (1, LANES), 1)
    zero = jnp.zeros_like(q)
    q_lo = jnp.where(lane < DIFF_QK_DIM, q, zero)
    q_hi = jnp.where(lane >= DIFF_QK_DIM, q, zero)
    maps = ((q_lo, m1, l1, a1), (q_hi, m2, l2, a2))
    for _, m, l, a in maps:
        m[...] = jnp.full(m.shape, NEG_BIG, F32)
        l[...] = jnp.zeros(l.shape, F32)
        a[...] = jnp.zeros(a.shape, F32)

    def chunk(kstart, diag_offset):
        kk = k_ref[pl.ds(kstart, tk), :]
        vv = v_ref[pl.ds(kstart, tk), :]
        for qq, m, l, a in maps:
            s = lax.dot_general(qq, kk, (((1,), (1,)), ((), ())), preferred_element_type=F32)
            if diag_offset is not None:
                r = lax.broadcasted_iota(jnp.int32, (tq, tk), 0)
                c = lax.broadcasted_iota(jnp.int32, (tq, tk), 1) + diag_offset
                s = jnp.where(c <= r, s, -jnp.inf)
            m_old = m[...]
            m_new = jnp.maximum(m_old, jnp.max(s, axis=1, keepdims=True))
            p = jnp.exp(s - m_new)
            alpha = jnp.exp(m_old - m_new)
            l[...] = alpha * l[...] + jnp.sum(p, axis=1, keepdims=True)
            a[...] = alpha * a[...] + jnp.dot(p.astype(BF16), vv, preferred_element_type=F32)
            m[...] = m_new

    per = tq // tk
    def body(i, carry):
        chunk(pl.multiple_of(i * tk, tk), None)
        return carry
    lax.fori_loop(0, qi * per, body, 0)
    for j in range(per):
        chunk(pl.multiple_of(qi * tq + j * tk, tk), j * tk)

    lam = (jnp.exp(jnp.sum(lq1[...] * lk1[...], axis=1, keepdims=True))
           - jnp.exp(jnp.sum(lq2[...] * lk2[...], axis=1, keepdims=True)) + lam_init)
    out = a1[...] / l1[...] - lam * (a2[...] / l2[...])
    o_ref[...] = (_rms(out, g_ref[...]) * (1.0 - lam_init)).astype(o_ref.dtype)


def _diff_attention(dq, dk, dv, lams, g_sub, bsz, seq, lam_init):
    m = bsz * seq
    tq = min(512, seq)
    tk = tq
    nq = seq // tq
    qspec = pl.BlockSpec((tq, LANES), lambda b, h, i: (b * nq + i, h))
    kvspec = pl.BlockSpec((seq, LANES), lambda b, h, i: (b, h))
    vec = lambda n: pl.BlockSpec((1, n), lambda b, h, i: (0, 0))
    kern = functools.partial(_diff_kernel, tq=tq, tk=tk, lam_init=lam_init)
    return pl.pallas_call(
        kern,
        grid=(bsz, N_DIFF_HEADS, nq),
        in_specs=[qspec, kvspec, kvspec] + [vec(DIFF_QK_DIM)] * 4 + [vec(DIFF_V_DIM)],
        out_specs=qspec,
        out_shape=jax.ShapeDtypeStruct((m, DIFF_V), BF16),
        scratch_shapes=[
            pltpu.VMEM((tq, 1), F32), pltpu.VMEM((tq, 1), F32), pltpu.VMEM((tq, LANES), F32),
            pltpu.VMEM((tq, 1), F32), pltpu.VMEM((tq, 1), F32), pltpu.VMEM((tq, LANES), F32),
        ],
        compiler_params=_cparams(("parallel", "parallel", "arbitrary")),
        name="diff",
    )(dq, dk, dv, *lams, g_sub)


def _dsa_kernel(iq_ref, iw_ref, ik_ref, q_ref, k_ref, v_ref, o_ref,
                keys, m_s, l_s, acc, *, tq, tk, k_top, seq):
    qi = pl.program_id(1)
    n_chunks = (qi * tq + tq + tk - 1) // tk
    row = qi * tq + lax.broadcasted_iota(jnp.int32, (tq, tk), 0)
    col0 = lax.broadcasted_iota(jnp.int32, (tq, tk), 1)
    lane = lax.broadcasted_iota(jnp.int32, (1, LANES), 1)

    iq = iq_ref[...]
    iw = iw_ref[...]
    zero = jnp.zeros((tq, LANES), BF16)
    halves = []
    for j in range(N_IDX_HEADS // 2):
        pair = iq[:, j * LANES:(j + 1) * LANES]
        halves.append(jnp.where(lane < IDX_DIM, pair, zero))
        halves.append(jnp.where(lane >= IDX_DIM, pair, zero))

    def score_chunk(c, carry):
        kstart = pl.multiple_of(c * tk, tk)
        ik = ik_ref[pl.ds(kstart, tk), :]
        sc = jnp.zeros((tq, tk), F32)
        for h in range(N_IDX_HEADS):
            lg = lax.dot_general(halves[h], ik, (((1,), (1,)), ((), ())),
                                 preferred_element_type=F32)
            sc = sc + iw[:, h:h + 1] * jnp.maximum(lg, 0.0)
        bits = lax.bitcast_convert_type(sc, jnp.int32)
        key = bits ^ ((bits >> 31) & jnp.int32(0x7FFFFFFF))
        key = jnp.where(col0 + kstart <= row, key, jnp.int32(INT_MIN))
        keys[:, pl.ds(kstart, tk)] = key
        return carry
    lax.fori_loop(0, n_chunks, score_chunk, 0)

    def count_ge(thr):
        def body(c, cnt):
            kstart = pl.multiple_of(c * tk, tk)
            kc = keys[:, pl.ds(kstart, tk)]
            hit = jnp.where(kc >= thr, 1, 0)
            part = hit[:, 0:LANES]
            for g in range(1, tk // LANES):
                part = part + hit[:, g * LANES:(g + 1) * LANES]
            return cnt + part
        cnt = lax.fori_loop(0, n_chunks, body, jnp.zeros((tq, LANES), jnp.int32))
        return jnp.sum(cnt, axis=1, keepdims=True)

    def bit_step(it, tau):
        inc = jnp.left_shift(jnp.int32(1), 31 - it)
        cand = tau + inc
        return jnp.where(count_ge(cand) >= k_top, cand, tau)
    tau = lax.fori_loop(0, 32, bit_step, jnp.full((tq, 1), INT_MIN, jnp.int32))

    n_ge = count_ge(tau)
    live = tau > INT_MIN
    tied = jnp.logical_and(n_ge > k_top, live)
    any_tied = jnp.max(jnp.where(tied, 1, 0)) > 0

    @pl.when(any_tied)
    def _():
        n_gt = count_ge(tau + 1)
        need = k_top - n_gt

        def count_eq_below(j):
            def body(c, cnt):
                kstart = pl.multiple_of(c * tk, tk)
                kc = keys[:, pl.ds(kstart, tk)]
                hit = jnp.where(jnp.logical_and(kc == tau, col0 + kstart < j), 1, 0)
                return cnt + jnp.sum(hit, axis=1, keepdims=True)
            return lax.fori_loop(0, n_chunks, body, jnp.zeros((tq, 1), jnp.int32))

        nbits = max(1, (seq - 1).bit_length())
        def idx_step(it, j):
            cand = j + jnp.left_shift(jnp.int32(1), nbits - 1 - it)
            return jnp.where(count_eq_below(cand) < need, cand, j)
        jmax = lax.fori_loop(0, nbits, idx_step, jnp.zeros((tq, 1), jnp.int32))

        def demote(c, carry):
            kstart = pl.multiple_of(c * tk, tk)
            kc = keys[:, pl.ds(kstart, tk)]
            drop = jnp.logical_and(jnp.logical_and(kc == tau, col0 + kstart > jmax), tied)
            keys[:, pl.ds(kstart, tk)] = jnp.where(drop, jnp.int32(INT_MIN), kc)
            return carry
        lax.fori_loop(0, n_chunks, demote, 0)

    thr = jnp.where(live, tau, jnp.int32(INT_MIN + 1))

    nh = N_DSA_HEADS
    q_all = q_ref[...].reshape(nh * tq, DSA_HEAD_DIM)
    m_s[...] = jnp.full(m_s.shape, NEG_BIG, F32)
    l_s[...] = jnp.zeros(l_s.shape, F32)
    acc[...] = jnp.zeros(acc.shape, F32)

    def attn_chunk(c, carry):
        kstart = pl.multiple_of(c * tk, tk)
        kk = k_ref[pl.ds(kstart, tk), :]
        vv = v_ref[pl.ds(kstart, tk), :]
        sel = keys[:, pl.ds(kstart, tk)] >= thr
        s = lax.dot_general(q_all, kk, (((1,), (1,)), ((), ())), preferred_element_type=F32)
        s = jnp.where(sel[None], s.reshape(nh, tq, tk), -jnp.inf).reshape(nh * tq, tk)
        m_old = m_s[...]
        m_new = jnp.maximum(m_old, jnp.max(s, axis=1, keepdims=True))
        p = jnp.exp(s - m_new)
        alpha = jnp.exp(m_old - m_new)
        l_s[...] = alpha * l_s[...] + jnp.sum(p, axis=1, keepdims=True)
        acc[...] = alpha * acc[...] + jnp.dot(p.astype(BF16), vv, preferred_element_type=F32)
        m_s[...] = m_new
        return carry
    lax.fori_loop(0, n_chunks, attn_chunk, 0)

    out = (acc[...] / l_s[...]).reshape(nh, tq, DSA_HEAD_DIM)
    for h in range(nh):
        o_ref[:, h * DSA_HEAD_DIM:(h + 1) * DSA_HEAD_DIM] = out[h].astype(o_ref.dtype)


def _dsa_attention(iq, iw, ik2, sq, sk, sv, bsz, seq, k_top):
    m = bsz * seq
    tq = 128
    tk = min(512, seq)
    nq = seq // tq
    rowspec = lambda n: pl.BlockSpec((tq, n), lambda b, i: (b * nq + i, 0))
    seqspec = pl.BlockSpec((seq, LANES), lambda b, i: (b, 0))
    kern = functools.partial(_dsa_kernel, tq=tq, tk=tk, k_top=k_top, seq=seq)
    return pl.pallas_call(
        kern,
        grid=(bsz, nq),
        in_specs=[
            rowspec(IDX_Q), rowspec(LANES), seqspec,
            pl.BlockSpec((N_DSA_HEADS, tq, DSA_HEAD_DIM), lambda b, i: (0, b * nq + i, 0)),
            seqspec, seqspec,
        ],
        out_specs=rowspec(DSA_Q),
        out_shape=jax.ShapeDtypeStruct((m, DSA_Q), BF16),
        scratch_shapes=[
            pltpu.VMEM((tq, seq), jnp.int32),
            pltpu.VMEM((N_DSA_HEADS * tq, 1), F32),
            pltpu.VMEM((N_DSA_HEADS * tq, 1), F32),
            pltpu.VMEM((N_DSA_HEADS * tq, DSA_HEAD_DIM), F32),
        ],
        compiler_params=_cparams(("parallel", "arbitrary")),
        name="dsa",
    )(iq, iw, ik2, sq, sk, sv)


def _out_kernel(a_ref, b_ref, wo_ref, x_ref, mod_ref, gpost_ref, gpre_ref, x1_ref, h2_ref):
    ka = a_ref.shape[1]
    mixed = (jnp.dot(a_ref[...], wo_ref[0:ka, :], preferred_element_type=F32)
             + jnp.dot(b_ref[...], wo_ref[ka:, :], preferred_element_type=F32))
    x1 = x_ref[...] + mod_ref[0, 2:3, :] * _rms(mixed, gpost_ref[...])
    x1_ref[...] = x1
    h2 = _rms(x1, gpre_ref[...]) * (1.0 + mod_ref[0, 4:5, :]) + mod_ref[0, 3:4, :]
    h2_ref[...] = h2.astype(h2_ref.dtype)


def _out_proj(diff_out, dsa_out, wo, x2, mod3, g_post, g_pre, seq):
    m, d = x2.shape
    tm = min(512, seq)
    tiles_per_batch = seq // tm
    row = lambda n: pl.BlockSpec((tm, n), lambda i: (i, 0))
    return pl.pallas_call(
        _out_kernel,
        grid=(m // tm,),
        in_specs=[
            row(diff_out.shape[1]), row(dsa_out.shape[1]),
            _resident(wo.shape, lambda i: (0, 0)),
            row(d),
            pl.BlockSpec((1, ADA_CHUNKS, d), lambda i: (i // tiles_per_batch, 0, 0)),
            _resident((1, d), lambda i: (0, 0)),
            _resident((1, d), lambda i: (0, 0)),
        ],
        out_specs=[row(d), row(d)],
        out_shape=[jax.ShapeDtypeStruct((m, d), F32), jax.ShapeDtypeStruct((m, d), BF16)],
        compiler_params=_cparams(("parallel",)),
        name="out",
    )(diff_out, dsa_out, wo, x2, mod3, g_post, g_pre)


def _ffn_kernel(h_ref, wg_ref, wu_ref, wd_ref, x1_ref, mod_ref, g_ref, o_ref, acc):
    f = pl.program_id(1)

    @pl.when(f == 0)
    def _():
        acc[...] = jnp.zeros(acc.shape, F32)

    h = h_ref[...]
    gate = jnp.dot(h, wg_ref[...], preferred_element_type=F32)
    up = jnp.dot(h, wu_ref[...], preferred_element_type=F32)
    act = (gate * jax.nn.sigmoid(gate) * up).astype(BF16)
    acc[...] += jnp.dot(act, wd_ref[...], preferred_element_type=F32)

    @pl.when(f == pl.num_programs(1) - 1)
    def _():
        o_ref[...] = x1_ref[...] + mod_ref[0, 5:6, :] * _rms(acc[...], g_ref[...])


def _ffn(h2, wg, wu, wd, x1, mod3, g_post, seq):
    m, d = x1.shape
    dff = wg.shape[1]
    tm = min(512, seq)
    tf = 512
    tiles_per_batch = seq // tm
    row = lambda dt: pl.BlockSpec((tm, d), lambda i, f: (i, 0))
    return pl.pallas_call(
        _ffn_kernel,
        grid=(m // tm, dff // tf),
        in_specs=[
            row(BF16),
            pl.BlockSpec((d, tf), lambda i, f: (0, f)),
            pl.BlockSpec((d, tf), lambda i, f: (0, f)),
            pl.BlockSpec((tf, d), lambda i, f: (f, 0)),
            row(F32),
            pl.BlockSpec((1, ADA_CHUNKS, d), lambda i, f: (i // tiles_per_batch, 0, 0)),
            _resident((1, d), lambda i, f: (0, 0)),
        ],
        out_specs=row(F32),
        out_shape=jax.ShapeDtypeStruct((m, d), F32),
        scratch_shapes=[pltpu.VMEM((tm, d), F32)],
        compiler_params=_cparams(("parallel", "arbitrary")),
        name="ffn",
    )(h2, wg, wu, wd, x1, mod3, g_post)


def _split_w_in(w_in):
    offs = [0]
    for sz in PROJ_SIZES:
        offs.append(offs[-1] + sz)
    p = [w_in[:, offs[i]:offs[i + 1]] for i in range(len(PROJ_SIZES))]
    d = w_in.shape[0]
    wik2 = jnp.concatenate([p[7], p[7]], axis=1)
    wiw = jnp.concatenate([p[8], jnp.zeros((d, LANES - IDX_W), w_in.dtype)], axis=1)
    parts = p[:7] + [wik2, wiw]
    return [w.astype(BF16) for w in parts]


def kernel(x, c, positions, w_ada, b_ada, g_attn_pre, g_attn_post, g_ffn_pre, g_ffn_post, w_in, lambda_q1, lambda_k1, lambda_q2, lambda_k2, g_diff_sub, w_o, w_gate, w_up, w_down):
    bsz, seq, d = x.shape
    depth = w_ada.shape[0]
    k_top = min(INDEX_TOPK, seq // 4)
    m = bsz * seq
    tabs = _rope_tables(positions)
    x2 = x.reshape(m, d)
    for l in range(depth):
        lam_init = 0.8 - 0.6 * math.exp(-0.3 * l)
        mod3 = _ada(c, w_ada[l], b_ada[l]).reshape(bsz, ADA_CHUNKS, d)
        vec = lambda a: a[l].reshape(1, -1)
        odq, odk, odv, osq, osk, osv, oiq, oik, oiw = _proj(
            x2, vec(g_attn_pre), mod3, tabs, _split_w_in(w_in[l]), seq)
        lams = [vec(lambda_q1), vec(lambda_k1), vec(lambda_q2), vec(lambda_k2)]
        diff_out = _diff_attention(odq, odk, odv, lams, vec(g_diff_sub), bsz, seq, lam_init)
        dsa_out = _dsa_attention(oiq, oiw, oik, osq, osk, osv, bsz, seq, k_top)
        x1, h2 = _out_proj(diff_out, dsa_out, w_o[l].astype(BF16), x2, mod3,
                           vec(g_attn_post), vec(g_ffn_pre), seq)
        x2 = _ffn(h2, w_gate[l].astype(BF16), w_up[l].astype(BF16), w_down[l].astype(BF16),
                  x1, mod3, vec(g_ffn_post), seq)
    return x2.reshape(bsz, seq, d)
```

```python
import functools
import math

import jax
import jax.numpy as jnp
from jax import lax
from jax.experimental import pallas as pl
from jax.experimental.pallas import tpu as pltpu

N_DIFF_HEADS = 8
DIFF_QK_DIM = 64
DIFF_V_DIM = 128
N_DSA_HEADS = 8
DSA_HEAD_DIM = 128
N_IDX_HEADS = 16
IDX_DIM = 64
INDEX_TOPK = 256
ROPE_THETA = 500000.0
ROPE_FRACTION = 4
ADA_CHUNKS = 6
RMS_EPS = 1e-6

DIFF_Q = N_DIFF_HEADS * 2 * DIFF_QK_DIM
DIFF_K = DIFF_Q
DIFF_V = N_DIFF_HEADS * DIFF_V_DIM
DSA_Q = N_DSA_HEADS * DSA_HEAD_DIM
DSA_K = DSA_HEAD_DIM
DSA_V = DSA_HEAD_DIM
IDX_Q = N_IDX_HEADS * IDX_DIM
IDX_K = IDX_DIM
IDX_W = N_IDX_HEADS
PROJ_SIZES = (DIFF_Q, DIFF_K, DIFF_V, DSA_Q, DSA_K, DSA_V, IDX_Q, IDX_K, IDX_W)

LANES = 128
MXU_WIDTH = 256
VMEM_LIMIT_BYTES = 56 * 1024 * 1024

F32 = jnp.float32
BF16 = jnp.bfloat16
INT_MIN = -(2 ** 31)
NEG_BIG = -1e30
LOG2E = math.log2(math.e)


def _cparams(sem):
    return pltpu.CompilerParams(dimension_semantics=sem, vmem_limit_bytes=VMEM_LIMIT_BYTES)


def _resident(shape, index_map):
    return pl.BlockSpec(shape, index_map, pipeline_mode=pl.Buffered(1))


def _rms(x, g):
    ms = jnp.mean(x * x, axis=-1, keepdims=True)
    return x * lax.rsqrt(ms + RMS_EPS) * g


def _ada_kernel(ct_ref, w_ref, b_ref, o_ref):
    k = pl.program_id(1)

    @pl.when(k == 0)
    def _():
        o_ref[...] = jnp.broadcast_to(b_ref[...], o_ref.shape)

    cv = ct_ref[...]
    s = cv * jax.nn.sigmoid(cv)
    w = w_ref[...]
    for b in range(o_ref.shape[0]):
        o_ref[b:b + 1, :] += jnp.sum(w * s[:, b:b + 1], axis=0, keepdims=True)


def _ada(c, w_ada, b_ada):
    bsz, d = c.shape
    n = w_ada.shape[1]
    tk, tn = 512, 2048
    return pl.pallas_call(
        _ada_kernel,
        grid=(n // tn, d // tk),
        in_specs=[
            pl.BlockSpec((tk, bsz), lambda j, k: (k, 0)),
            pl.BlockSpec((tk, tn), lambda j, k: (k, j)),
            pl.BlockSpec((1, tn), lambda j, k: (0, j)),
        ],
        out_specs=pl.BlockSpec((bsz, tn), lambda j, k: (0, j)),
        out_shape=jax.ShapeDtypeStruct((bsz, n), F32),
        compiler_params=_cparams(("parallel", "arbitrary")),
        name="ada",
    )(c.T, w_ada, b_ada.reshape(1, n))


def _rope_kernel(pos_ref, c64_ref, s64_ref, c128_ref, s128_ref):
    pos = pos_ref[...]
    lane = lax.broadcasted_iota(jnp.int32, (1, LANES), 1)
    for hd, c_ref, s_ref in ((64, c64_ref, s64_ref), (128, c128_ref, s128_ref)):
        rot = hd // ROPE_FRACTION
        half = rot // 2
        c = lane % hd
        k = (c % half).astype(F32)
        inv = jnp.power(jnp.float32(ROPE_THETA), -(k * 2.0 / rot))
        inv = jnp.where(c < rot, inv, 0.0)
        ang = pos * inv
        c_ref[...] = jnp.cos(ang)
        s_ref[...] = jnp.sin(ang) * jnp.where(c < half, -1.0, 1.0)


def _rope_tables(positions):
    bsz, s = positions.shape
    m = bsz * s
    posb = jnp.broadcast_to(positions.reshape(m, 1).astype(F32), (m, LANES))
    tm = min(1024, m)
    spec = pl.BlockSpec((tm, LANES), lambda i: (i, 0))
    shp = jax.ShapeDtypeStruct((m, LANES), F32)
    return pl.pallas_call(
        _rope_kernel,
        grid=(m // tm,),
        in_specs=[spec],
        out_specs=[spec] * 4,
        out_shape=[shp] * 4,
        compiler_params=_cparams(("parallel",)),
        name="rope",
    )(posb)


def _rope_apply(y, c_tab, s_tab, hd):
    n = y.shape[1]
    reps = n // LANES
    half = hd // ROPE_FRACTION // 2
    lane = lax.broadcasted_iota(jnp.int32, (1, n), 1) % hd
    cc = jnp.tile(c_tab, (1, reps)) if reps > 1 else c_tab
    ss = jnp.tile(s_tab, (1, reps)) if reps > 1 else s_tab
    ahead = pltpu.roll(y, n - half, axis=1)
    behind = pltpu.roll(y, half, axis=1)
    partner = jnp.where(lane < half, ahead, behind)
    return y * cc + partner * ss


def _proj_kernel(x_ref, g_ref, mod_ref, c64_ref, s64_ref, c128_ref, s128_ref,
                 wdq, wdk, wdv, wsq, wsk, wsv, wiq, wik, wiw,
                 odq, odk, odv, osq, osk, osv, oiq, oik, oiw):
    x = x_ref[...]
    h = _rms(x, g_ref[...]) * (1.0 + mod_ref[0, 1:2, :]) + mod_ref[0, 0:1, :]
    hb = h.astype(BF16)
    c64, s64 = c64_ref[...], s64_ref[...]
    c128, s128 = c128_ref[...], s128_ref[...]

    def part(w_ref, fn, store):
        n = w_ref.shape[1]
        for c0 in range(0, n, MXU_WIDTH):
            cw = min(MXU_WIDTH, n - c0)
            y = jnp.dot(hb, w_ref[:, c0:c0 + cw], preferred_element_type=F32)
            store(c0, cw, fn(y))

    def flat_store(o_ref):
        def st(c0, cw, y):
            o_ref[:, c0:c0 + cw] = y.astype(o_ref.dtype)
        return st

    def head_store(o_ref):
        def st(c0, cw, y):
            for j in range(cw // DSA_HEAD_DIM):
                hh = c0 // DSA_HEAD_DIM + j
                o_ref[hh] = y[:, j * DSA_HEAD_DIM:(j + 1) * DSA_HEAD_DIM].astype(o_ref.dtype)
        return st

    rope64 = lambda y: _rope_apply(y, c64, s64, 64)
    rope128 = lambda y: _rope_apply(y, c128, s128, 128)
    diff_scale = DIFF_QK_DIM ** -0.5 * LOG2E
    dsa_scale = DSA_HEAD_DIM ** -0.5 * LOG2E
    idx_w_scale = (N_IDX_HEADS ** -0.5) * (IDX_DIM ** -0.5)

    part(wdq, lambda y: rope64(y) * diff_scale, flat_store(odq))
    part(wdk, rope64, flat_store(odk))
    part(wdv, lambda y: y, flat_store(odv))
    part(wsq, lambda y: rope128(y) * dsa_scale, head_store(osq))
    part(wsk, rope128, flat_store(osk))
    part(wsv, lambda y: y, flat_store(osv))
    part(wiq, rope64, flat_store(oiq))
    part(wik, rope64, flat_store(oik))
    part(wiw, lambda y: y * idx_w_scale, flat_store(oiw))


def _proj(x2, g, mod3, tabs, wparts, seq):
    m, d = x2.shape
    tm = min(512, seq)
    tiles_per_batch = seq // tm
    row = lambda n: pl.BlockSpec((tm, n), lambda i: (i, 0))
    in_specs = [
        row(d),
        _resident((1, d), lambda i: (0, 0)),
        pl.BlockSpec((1, ADA_CHUNKS, d), lambda i: (i // tiles_per_batch, 0, 0)),
        row(LANES), row(LANES), row(LANES), row(LANES),
    ] + [_resident(w.shape, lambda i: (0, 0)) for w in wparts]
    out_specs = [
        row(DIFF_Q), row(DIFF_K), row(DIFF_V),
        pl.BlockSpec((N_DSA_HEADS, tm, DSA_HEAD_DIM), lambda i: (0, i, 0)),
        row(LANES), row(LANES), row(IDX_Q), row(LANES), row(LANES),
    ]
    sds = jax.ShapeDtypeStruct
    out_shape = [
        sds((m, DIFF_Q), BF16), sds((m, DIFF_K), BF16), sds((m, DIFF_V), BF16),
        sds((N_DSA_HEADS, m, DSA_HEAD_DIM), BF16),
        sds((m, LANES), BF16), sds((m, LANES), BF16), sds((m, IDX_Q), BF16),
        sds((m, LANES), BF16), sds((m, LANES), F32),
    ]
    return pl.pallas_call(
        _proj_kernel,
        grid=(m // tm,),
        in_specs=in_specs,
        out_specs=out_specs,
        out_shape=out_shape,
        compiler_params=_cparams(("parallel",)),
        name="proj",
    )(x2, g, mod3, *tabs, *wparts)


def _diff_kernel(q_ref, k_ref, v_ref, lq1, lk1, lq2, lk2, g_ref, o_ref,
                 m1, l1, a1, m2, l2, a2, *, tq, tk, lam_init):
    qi = pl.program_id(2)
    q = q_ref[...]
    lane = lax.broadcasted_iota(jnp.int32, (1, LANES), 1)
    zero = jnp.zeros_like(q)
    q_lo = jnp.where(lane < DIFF_QK_DIM, q, zero)
    q_hi = jnp.where(lane >= DIFF_QK_DIM, q, zero)
    maps = ((q_lo, m1, l1, a1), (q_hi, m2, l2, a2))
    for _, m, l, a in maps:
        m[...] = jnp.full(m.shape, NEG_BIG, F32)
        l[...] = jnp.zeros(l.shape, F32)
        a[...] = jnp.zeros(a.shape, F32)

    def chunk(kstart, diag_offset):
        kk = k_ref[pl.ds(kstart, tk), :]
        vv = v_ref[pl.ds(kstart, tk), :]
        for qq, m, l, a in maps:
            s = lax.dot_general(qq, kk, (((1,), (1,)), ((), ())), preferred_element_type=F32)
            if diag_offset is not None:
                r = lax.broadcasted_iota(jnp.int32, (tq, tk), 0)
                c = lax.broadcasted_iota(jnp.int32, (tq, tk), 1) + diag_offset
                s = jnp.where(c <= r, s, -jnp.inf)
            m_old = m[...]
            m_new = jnp.maximum(m_old, jnp.max(s, axis=1, keepdims=True))
            p = jnp.exp2(s - jnp.tile(m_new, (1, tk // LANES)))
            alpha = jnp.exp2(m_old - m_new)
            l[...] = alpha * l[...] + jnp.sum(p, axis=1, keepdims=True)
            a[...] = alpha * a[...] + jnp.dot(p.astype(BF16), vv, preferred_element_type=F32)
            m[...] = m_new

    per = tq // tk
    def body(i, carry):
        chunk(pl.multiple_of(i * tk, tk), None)
        return carry
    lax.fori_loop(0, qi * per, body, 0)
    for j in range(per):
        chunk(pl.multiple_of(qi * tq + j * tk, tk), j * tk)

    lam = (jnp.exp(jnp.sum(lq1[...] * lk1[...], axis=1, keepdims=True))
           - jnp.exp(jnp.sum(lq2[...] * lk2[...], axis=1, keepdims=True)) + lam_init)
    out = a1[...] / l1[...] - lam * (a2[...] / l2[...])
    o_ref[...] = (_rms(out, g_ref[...]) * (1.0 - lam_init)).astype(o_ref.dtype)


def _diff_attention(dq, dk, dv, lams, g_sub, bsz, seq, lam_init):
    m = bsz * seq
    tq = min(512, seq)
    tk = tq
    nq = seq // tq
    qspec = pl.BlockSpec((tq, LANES), lambda b, h, i: (b * nq + i, h))
    kvspec = pl.BlockSpec((seq, LANES), lambda b, h, i: (b, h))
    vec = lambda n: pl.BlockSpec((1, n), lambda b, h, i: (0, 0))
    kern = functools.partial(_diff_kernel, tq=tq, tk=tk, lam_init=lam_init)
    return pl.pallas_call(
        kern,
        grid=(bsz, N_DIFF_HEADS, nq),
        in_specs=[qspec, kvspec, kvspec] + [vec(DIFF_QK_DIM)] * 4 + [vec(DIFF_V_DIM)],
        out_specs=qspec,
        out_shape=jax.ShapeDtypeStruct((m, DIFF_V), BF16),
        scratch_shapes=[
            pltpu.VMEM((tq, LANES), F32), pltpu.VMEM((tq, LANES), F32), pltpu.VMEM((tq, LANES), F32),
            pltpu.VMEM((tq, LANES), F32), pltpu.VMEM((tq, LANES), F32), pltpu.VMEM((tq, LANES), F32),
        ],
        compiler_params=_cparams(("parallel", "parallel", "arbitrary")),
        name="diff",
    )(dq, dk, dv, *lams, g_sub)


def _dsa_kernel(iq_ref, iw_ref, ik_ref, q_ref, k_ref, v_ref, o_ref,
                keys, m_s, l_s, acc, *, tq, tk, k_top, seq):
    qi = pl.program_id(1)
    n_chunks = (qi * tq + tq + tk - 1) // tk
    row = qi * tq + lax.broadcasted_iota(jnp.int32, (tq, tk), 0)
    col0 = lax.broadcasted_iota(jnp.int32, (tq, tk), 1)
    lane = lax.broadcasted_iota(jnp.int32, (1, LANES), 1)

    iq = iq_ref[...]
    iw = iw_ref[...]
    zero = jnp.zeros((tq, LANES), BF16)
    halves = []
    for j in range(N_IDX_HEADS // 2):
        pair = iq[:, j * LANES:(j + 1) * LANES]
        halves.append(jnp.where(lane < IDX_DIM, pair, zero))
        halves.append(jnp.where(lane >= IDX_DIM, pair, zero))

    def score_chunk(c, carry):
        kstart = pl.multiple_of(c * tk, tk)
        ik = ik_ref[pl.ds(kstart, tk), :]
        sc = jnp.zeros((tq, tk), F32)
        for h in range(N_IDX_HEADS):
            lg = lax.dot_general(halves[h], ik, (((1,), (1,)), ((), ())),
                                 preferred_element_type=F32)
            sc = sc + iw[:, h:h + 1] * jnp.maximum(lg, 0.0)
        bits = lax.bitcast_convert_type(sc, jnp.int32)
        key = bits ^ ((bits >> 31) & jnp.int32(0x7FFFFFFF))
        key = jnp.where(col0 + kstart <= row, key, jnp.int32(INT_MIN))
        keys[:, pl.ds(kstart, tk)] = key
        return carry
    lax.fori_loop(0, n_chunks, score_chunk, 0)

    def count_ge(thr):
        def body(c, cnt):
            kstart = pl.multiple_of(c * tk, tk)
            kc = keys[:, pl.ds(kstart, tk)]
            hit = jnp.where(kc >= thr, 1, 0)
            part = hit[:, 0:LANES]
            for g in range(1, tk // LANES):
                part = part + hit[:, g * LANES:(g + 1) * LANES]
            return cnt + part
        cnt = lax.fori_loop(0, n_chunks, body, jnp.zeros((tq, LANES), jnp.int32))
        return jnp.sum(cnt, axis=1, keepdims=True)

    def bit_step(it, tau):
        inc = jnp.left_shift(jnp.int32(1), 31 - it)
        cand = tau + inc
        return jnp.where(count_ge(cand) >= k_top, cand, tau)
    tau = lax.fori_loop(0, 32, bit_step, jnp.full((tq, 1), INT_MIN, jnp.int32))

    n_ge = count_ge(tau)
    live = tau > INT_MIN
    tied = jnp.logical_and(n_ge > k_top, live)
    any_tied = jnp.max(jnp.where(tied, 1, 0)) > 0

    @pl.when(any_tied)
    def _():
        n_gt = count_ge(tau + 1)
        need = k_top - n_gt

        def count_eq_below(j):
            def body(c, cnt):
                kstart = pl.multiple_of(c * tk, tk)
                kc = keys[:, pl.ds(kstart, tk)]
                hit = jnp.where(jnp.logical_and(kc == tau, col0 + kstart < j), 1, 0)
                return cnt + jnp.sum(hit, axis=1, keepdims=True)
            return lax.fori_loop(0, n_chunks, body, jnp.zeros((tq, 1), jnp.int32))

        nbits = max(1, (seq - 1).bit_length())
        def idx_step(it, j):
            cand = j + jnp.left_shift(jnp.int32(1), nbits - 1 - it)
            return jnp.where(count_eq_below(cand) < need, cand, j)
        jmax = lax.fori_loop(0, nbits, idx_step, jnp.zeros((tq, 1), jnp.int32))

        def demote(c, carry):
            kstart = pl.multiple_of(c * tk, tk)
            kc = keys[:, pl.ds(kstart, tk)]
            drop = jnp.logical_and(jnp.logical_and(kc == tau, col0 + kstart > jmax), tied)
            keys[:, pl.ds(kstart, tk)] = jnp.where(drop, jnp.int32(INT_MIN), kc)
            return carry
        lax.fori_loop(0, n_chunks, demote, 0)

    thr = jnp.where(live, tau, jnp.int32(INT_MIN + 1))

    nh = N_DSA_HEADS
    q_all = q_ref[...].reshape(nh * tq, DSA_HEAD_DIM)
    m_s[...] = jnp.full(m_s.shape, NEG_BIG, F32)
    l_s[...] = jnp.zeros(l_s.shape, F32)
    acc[...] = jnp.zeros(acc.shape, F32)

    def attn_chunk(c, carry):
        kstart = pl.multiple_of(c * tk, tk)
        kk = k_ref[pl.ds(kstart, tk), :]
        vv = v_ref[pl.ds(kstart, tk), :]
        sel = keys[:, pl.ds(kstart, tk)] >= thr
        s = lax.dot_general(q_all, kk, (((1,), (1,)), ((), ())), preferred_element_type=F32)
        s = jnp.where(sel[None], s.reshape(nh, tq, tk), -jnp.inf).reshape(nh * tq, tk)
        m_old = m_s[...]
        m_new = jnp.maximum(m_old, jnp.max(s, axis=1, keepdims=True))
        p = jnp.exp2(s - jnp.tile(m_new, (1, tk // LANES)))
        alpha = jnp.exp2(m_old - m_new)
        l_s[...] = alpha * l_s[...] + jnp.sum(p, axis=1, keepdims=True)
        acc[...] = alpha * acc[...] + jnp.dot(p.astype(BF16), vv, preferred_element_type=F32)
        m_s[...] = m_new
        return carry
    lax.fori_loop(0, n_chunks, attn_chunk, 0)

    out = (acc[...] / l_s[...]).reshape(nh, tq, DSA_HEAD_DIM)
    for h in range(nh):
        o_ref[:, h * DSA_HEAD_DIM:(h + 1) * DSA_HEAD_DIM] = out[h].astype(o_ref.dtype)


def _dsa_attention(iq, iw, ik2, sq, sk, sv, bsz, seq, k_top):
    m = bsz * seq
    tq = 128
    tk = min(512, seq)
    nq = seq // tq
    rowspec = lambda n: pl.BlockSpec((tq, n), lambda b, i: (b * nq + i, 0))
    seqspec = pl.BlockSpec((seq, LANES), lambda b, i: (b, 0))
    kern = functools.partial(_dsa_kernel, tq=tq, tk=tk, k_top=k_top, seq=seq)
    return pl.pallas_call(
        kern,
        grid=(bsz, nq),
        in_specs=[
            rowspec(IDX_Q), rowspec(LANES), seqspec,
            pl.BlockSpec((N_DSA_HEADS, tq, DSA_HEAD_DIM), lambda b, i: (0, b * nq + i, 0)),
            seqspec, seqspec,
        ],
        out_specs=rowspec(DSA_Q),
        out_shape=jax.ShapeDtypeStruct((m, DSA_Q), BF16),
        scratch_shapes=[
            pltpu.VMEM((tq, seq), jnp.int32),
            pltpu.VMEM((N_DSA_HEADS * tq, LANES), F32),
            pltpu.VMEM((N_DSA_HEADS * tq, LANES), F32),
            pltpu.VMEM((N_DSA_HEADS * tq, DSA_HEAD_DIM), F32),
        ],
        compiler_params=_cparams(("parallel", "arbitrary")),
        name="dsa",
    )(iq, iw, ik2, sq, sk, sv)


def _out_kernel(a_ref, b_ref, wo_ref, x_ref, mod_ref, gpost_ref, gpre_ref, x1_ref, h2_ref):
    ka = a_ref.shape[1]
    mixed = (jnp.dot(a_ref[...], wo_ref[0:ka, :], preferred_element_type=F32)
             + jnp.dot(b_ref[...], wo_ref[ka:, :], preferred_element_type=F32))
    x1 = x_ref[...] + mod_ref[0, 2:3, :] * _rms(mixed, gpost_ref[...])
    x1_ref[...] = x1
    h2 = _rms(x1, gpre_ref[...]) * (1.0 + mod_ref[0, 4:5, :]) + mod_ref[0, 3:4, :]
    h2_ref[...] = h2.astype(h2_ref.dtype)


def _out_proj(diff_out, dsa_out, wo, x2, mod3, g_post, g_pre, seq):
    m, d = x2.shape
    tm = min(512, seq)
    tiles_per_batch = seq // tm
    row = lambda n: pl.BlockSpec((tm, n), lambda i: (i, 0))
    return pl.pallas_call(
        _out_kernel,
        grid=(m // tm,),
        in_specs=[
            row(diff_out.shape[1]), row(dsa_out.shape[1]),
            _resident(wo.shape, lambda i: (0, 0)),
            row(d),
            pl.BlockSpec((1, ADA_CHUNKS, d), lambda i: (i // tiles_per_batch, 0, 0)),
            _resident((1, d), lambda i: (0, 0)),
            _resident((1, d), lambda i: (0, 0)),
        ],
        out_specs=[row(d), row(d)],
        out_shape=[jax.ShapeDtypeStruct((m, d), F32), jax.ShapeDtypeStruct((m, d), BF16)],
        compiler_params=_cparams(("parallel",)),
        name="out",
    )(diff_out, dsa_out, wo, x2, mod3, g_post, g_pre)


def _ffn_kernel(h_ref, wg_ref, wu_ref, wd_ref, x1_ref, mod_ref, g_ref, o_ref, acc):
    f = pl.program_id(1)

    @pl.when(f == 0)
    def _():
        acc[...] = jnp.zeros(acc.shape, F32)

    h = h_ref[...]
    gate = jnp.dot(h, wg_ref[...], preferred_element_type=F32)
    up = jnp.dot(h, wu_ref[...], preferred_element_type=F32)
    act = (gate * jax.nn.sigmoid(gate) * up).astype(BF16)
    acc[...] += jnp.dot(act, wd_ref[...], preferred_element_type=F32)

    @pl.when(f == pl.num_programs(1) - 1)
    def _():
        o_ref[...] = x1_ref[...] + mod_ref[0, 5:6, :] * _rms(acc[...], g_ref[...])


def _ffn(h2, wg, wu, wd, x1, mod3, g_post, seq):
    m, d = x1.shape
    dff = wg.shape[1]
    tm = min(512, seq)
    tf = 512
    tiles_per_batch = seq // tm
    row = lambda dt: pl.BlockSpec((tm, d), lambda i, f: (i, 0))
    return pl.pallas_call(
        _ffn_kernel,
        grid=(m // tm, dff // tf),
        in_specs=[
            row(BF16),
            pl.BlockSpec((d, tf), lambda i, f: (0, f)),
            pl.BlockSpec((d, tf), lambda i, f: (0, f)),
            pl.BlockSpec((tf, d), lambda i, f: (f, 0)),
            row(F32),
            pl.BlockSpec((1, ADA_CHUNKS, d), lambda i, f: (i // tiles_per_batch, 0, 0)),
            _resident((1, d), lambda i, f: (0, 0)),
        ],
        out_specs=row(F32),
        out_shape=jax.ShapeDtypeStruct((m, d), F32),
        scratch_shapes=[pltpu.VMEM((tm, d), F32)],
        compiler_params=_cparams(("parallel", "arbitrary")),
        name="ffn",
    )(h2, wg, wu, wd, x1, mod3, g_post)


def _split_w_in(w_in):
    offs = [0]
    for sz in PROJ_SIZES:
        offs.append(offs[-1] + sz)
    p = [w_in[:, offs[i]:offs[i + 1]] for i in range(len(PROJ_SIZES))]
    d = w_in.shape[0]
    wik2 = jnp.concatenate([p[7], p[7]], axis=1)
    wiw = jnp.concatenate([p[8], jnp.zeros((d, LANES - IDX_W), w_in.dtype)], axis=1)
    parts = p[:7] + [wik2, wiw]
    return [w.astype(BF16) for w in parts]


def kernel(x, c, positions, w_ada, b_ada, g_attn_pre, g_attn_post, g_ffn_pre, g_ffn_post, w_in, lambda_q1, lambda_k1, lambda_q2, lambda_k2, g_diff_sub, w_o, w_gate, w_up, w_down):
    bsz, seq, d = x.shape
    depth = w_ada.shape[0]
    k_top = min(INDEX_TOPK, seq // 4)
    m = bsz * seq
    tabs = _rope_tables(positions)
    x2 = x.reshape(m, d)
    for l in range(depth):
        lam_init = 0.8 - 0.6 * math.exp(-0.3 * l)
        mod3 = _ada(c, w_ada[l], b_ada[l]).reshape(bsz, ADA_CHUNKS, d)
        vec = lambda a: a[l].reshape(1, -1)
        odq, odk, odv, osq, osk, osv, oiq, oik, oiw = _proj(
            x2, vec(g_attn_pre), mod3, tabs, _split_w_in(w_in[l]), seq)
        lams = [vec(lambda_q1), vec(lambda_k1), vec(lambda_q2), vec(lambda_k2)]
        diff_out = _diff_attention(odq, odk, odv, lams, vec(g_diff_sub), bsz, seq, lam_init)
        dsa_out = _dsa_attention(oiq, oiw, oik, osq, osk, osv, bsz, seq, k_top)
        x1, h2 = _out_proj(diff_out, dsa_out, w_o[l].astype(BF16), x2, mod3,
                           vec(g_attn_post), vec(g_ffn_pre), seq)
        x2 = _ffn(h2, w_gate[l].astype(BF16), w_up[l].astype(BF16), w_down[l].astype(BF16),
                  x1, mod3, vec(g_ffn_post), seq)
    return x2.reshape(bsz, seq, d)
```

```python
import functools
import math

import jax
import jax.numpy as jnp
from jax import lax
from jax.experimental import pallas as pl
from jax.experimental.pallas import tpu as pltpu

N_DIFF_HEADS = 8
DIFF_QK_DIM = 64
DIFF_V_DIM = 128
N_DSA_HEADS = 8
DSA_HEAD_DIM = 128
N_IDX_HEADS = 16
IDX_DIM = 64
INDEX_TOPK = 256
ROPE_THETA = 500000.0
ROPE_FRACTION = 4
ADA_CHUNKS = 6
RMS_EPS = 1e-6

DIFF_Q = N_DIFF_HEADS * 2 * DIFF_QK_DIM
DIFF_K = DIFF_Q
DIFF_V = N_DIFF_HEADS * DIFF_V_DIM
DSA_Q = N_DSA_HEADS * DSA_HEAD_DIM
DSA_K = DSA_HEAD_DIM
DSA_V = DSA_HEAD_DIM
IDX_Q = N_IDX_HEADS * IDX_DIM
IDX_K = IDX_DIM
IDX_W = N_IDX_HEADS
PROJ_SIZES = (DIFF_Q, DIFF_K, DIFF_V, DSA_Q, DSA_K, DSA_V, IDX_Q, IDX_K, IDX_W)

LANES = 128
MXU_WIDTH = 256
VMEM_LIMIT_BYTES = 56 * 1024 * 1024

ADA_TK, ADA_TN = 512, 2048
ROPE_TM = 1024
ROW_TM = 512
FFN_TF = 512
DIFF_TQ = 1024
DIFF_TK = 1024
DIFF_HPB = 2
DSA_TQS = 512
DSA_TQA = 256
DSA_TK = 512

F32 = jnp.float32
BF16 = jnp.bfloat16
INT_MIN = -(2 ** 31)
NEG_BIG = -1e30
LOG2E = math.log2(math.e)


def _cparams(sem):
    return pltpu.CompilerParams(dimension_semantics=sem, vmem_limit_bytes=VMEM_LIMIT_BYTES)


def _resident(shape, index_map):
    return pl.BlockSpec(shape, index_map, pipeline_mode=pl.Buffered(1))


def _rms(x, g):
    ms = jnp.mean(x * x, axis=-1, keepdims=True)
    return x * lax.rsqrt(ms + RMS_EPS) * g


def _ada_kernel(ct_ref, w_ref, b_ref, o_ref):
    k = pl.program_id(1)

    @pl.when(k == 0)
    def _():
        o_ref[...] = jnp.broadcast_to(b_ref[...], o_ref.shape)

    cv = ct_ref[...]
    s = cv * jax.nn.sigmoid(cv)
    w = w_ref[...]
    for b in range(o_ref.shape[0]):
        o_ref[b:b + 1, :] += jnp.sum(w * s[:, b:b + 1], axis=0, keepdims=True)


def _ada(c, w_ada, b_ada):
    bsz, d = c.shape
    n = w_ada.shape[1]
    tk, tn = ADA_TK, ADA_TN
    return pl.pallas_call(
        _ada_kernel,
        grid=(n // tn, d // tk),
        in_specs=[
            pl.BlockSpec((tk, bsz), lambda j, k: (k, 0)),
            pl.BlockSpec((tk, tn), lambda j, k: (k, j)),
            pl.BlockSpec((1, tn), lambda j, k: (0, j)),
        ],
        out_specs=pl.BlockSpec((bsz, tn), lambda j, k: (0, j)),
        out_shape=jax.ShapeDtypeStruct((bsz, n), F32),
        compiler_params=_cparams(("parallel", "arbitrary")),
        name="ada",
    )(c.T, w_ada, b_ada.reshape(1, n))


def _rope_kernel(pos_ref, c64_ref, s64_ref, c128_ref, s128_ref):
    pos = pos_ref[...]
    lane = lax.broadcasted_iota(jnp.int32, (1, LANES), 1)
    for hd, c_ref, s_ref in ((64, c64_ref, s64_ref), (128, c128_ref, s128_ref)):
        rot = hd // ROPE_FRACTION
        half = rot // 2
        c = lane % hd
        k = (c % half).astype(F32)
        inv = jnp.power(jnp.float32(ROPE_THETA), -(k * 2.0 / rot))
        inv = jnp.where(c < rot, inv, 0.0)
        ang = pos * inv
        c_ref[...] = jnp.cos(ang)
        s_ref[...] = jnp.sin(ang) * jnp.where(c < half, -1.0, 1.0)


def _rope_tables(positions):
    bsz, s = positions.shape
    m = bsz * s
    posb = jnp.broadcast_to(positions.reshape(m, 1).astype(F32), (m, LANES))
    tm = min(ROPE_TM, m)
    spec = pl.BlockSpec((tm, LANES), lambda i: (i, 0))
    shp = jax.ShapeDtypeStruct((m, LANES), F32)
    return pl.pallas_call(
        _rope_kernel,
        grid=(m // tm,),
        in_specs=[spec],
        out_specs=[spec] * 4,
        out_shape=[shp] * 4,
        compiler_params=_cparams(("parallel",)),
        name="rope",
    )(posb)


def _rope_apply(y, c_tab, s_tab, hd):
    n = y.shape[1]
    reps = n // LANES
    half = hd // ROPE_FRACTION // 2
    lane = lax.broadcasted_iota(jnp.int32, (1, n), 1) % hd
    cc = jnp.tile(c_tab, (1, reps)) if reps > 1 else c_tab
    ss = jnp.tile(s_tab, (1, reps)) if reps > 1 else s_tab
    ahead = pltpu.roll(y, n - half, axis=1)
    behind = pltpu.roll(y, half, axis=1)
    partner = jnp.where(lane < half, ahead, behind)
    return y * cc + partner * ss


def _proj_kernel(x_ref, g_ref, mod_ref, c64_ref, s64_ref, c128_ref, s128_ref,
                 wdq, wdk, wdv, wsq, wsk, wsv, wiq, wik, wiw,
                 odq, odk, odv, osq, osk, osv, oiq, oik, oiw):
    x = x_ref[...]
    h = _rms(x, g_ref[...]) * (1.0 + mod_ref[0, 1:2, :]) + mod_ref[0, 0:1, :]
    hb = h.astype(BF16)
    c64, s64 = c64_ref[...], s64_ref[...]
    c128, s128 = c128_ref[...], s128_ref[...]

    def part(w_ref, fn, store):
        n = w_ref.shape[1]
        for c0 in range(0, n, MXU_WIDTH):
            cw = min(MXU_WIDTH, n - c0)
            y = jnp.dot(hb, w_ref[:, c0:c0 + cw], preferred_element_type=F32)
            store(c0, cw, fn(y))

    def flat_store(o_ref):
        def st(c0, cw, y):
            o_ref[:, c0:c0 + cw] = y.astype(o_ref.dtype)
        return st

    def head_store(o_ref):
        def st(c0, cw, y):
            for j in range(cw // DSA_HEAD_DIM):
                hh = c0 // DSA_HEAD_DIM + j
                o_ref[hh] = y[:, j * DSA_HEAD_DIM:(j + 1) * DSA_HEAD_DIM].astype(o_ref.dtype)
        return st

    rope64 = lambda y: _rope_apply(y, c64, s64, 64)
    rope128 = lambda y: _rope_apply(y, c128, s128, 128)
    diff_scale = DIFF_QK_DIM ** -0.5 * LOG2E
    dsa_scale = DSA_HEAD_DIM ** -0.5 * LOG2E
    idx_w_scale = (N_IDX_HEADS ** -0.5) * (IDX_DIM ** -0.5)

    part(wdq, lambda y: rope64(y) * diff_scale, flat_store(odq))
    part(wdk, rope64, flat_store(odk))
    part(wdv, lambda y: y, flat_store(odv))
    part(wsq, lambda y: rope128(y) * dsa_scale, head_store(osq))
    part(wsk, rope128, flat_store(osk))
    part(wsv, lambda y: y, flat_store(osv))
    part(wiq, rope64, flat_store(oiq))
    part(wik, rope64, flat_store(oik))
    part(wiw, lambda y: y * idx_w_scale, flat_store(oiw))


def _proj(x2, g, mod3, tabs, wparts, seq):
    m, d = x2.shape
    tm = min(ROW_TM, seq)
    tiles_per_batch = seq // tm
    row = lambda n: pl.BlockSpec((tm, n), lambda i: (i, 0))
    in_specs = [
        row(d),
        _resident((1, d), lambda i: (0, 0)),
        pl.BlockSpec((1, ADA_CHUNKS, d), lambda i: (i // tiles_per_batch, 0, 0)),
        row(LANES), row(LANES), row(LANES), row(LANES),
    ] + [_resident(w.shape, lambda i: (0, 0)) for w in wparts]
    out_specs = [
        row(DIFF_Q), row(DIFF_K), row(DIFF_V),
        pl.BlockSpec((N_DSA_HEADS, tm, DSA_HEAD_DIM), lambda i: (0, i, 0)),
        row(LANES), row(LANES), row(IDX_Q), row(LANES), row(LANES),
    ]
    sds = jax.ShapeDtypeStruct
    out_shape = [
        sds((m, DIFF_Q), BF16), sds((m, DIFF_K), BF16), sds((m, DIFF_V), BF16),
        sds((N_DSA_HEADS, m, DSA_HEAD_DIM), BF16),
        sds((m, LANES), BF16), sds((m, LANES), BF16), sds((m, IDX_Q), BF16),
        sds((m, LANES), BF16), sds((m, LANES), F32),
    ]
    return pl.pallas_call(
        _proj_kernel,
        grid=(m // tm,),
        in_specs=in_specs,
        out_specs=out_specs,
        out_shape=out_shape,
        compiler_params=_cparams(("parallel",)),
        name="proj",
    )(x2, g, mod3, *tabs, *wparts)


def _diff_kernel(q_ref, k_ref, v_ref, lq1, lk1, lq2, lk2, g_ref, o_ref,
                 m_s, l_s, a_s, *, tq, tk, hpb, lam_init):
    qi = pl.program_id(2)
    lane = lax.broadcasted_iota(jnp.int32, (1, LANES), 1)
    lt = tk // LANES
    qs = []
    for h in range(hpb):
        q = q_ref[:, h * LANES:(h + 1) * LANES]
        zero = jnp.zeros_like(q)
        qs.append(jnp.where(lane < DIFF_QK_DIM, q, zero))
        qs.append(jnp.where(lane >= DIFF_QK_DIM, q, zero))
    m_s[...] = jnp.full(m_s.shape, NEG_BIG, F32)
    l_s[...] = jnp.zeros(l_s.shape, F32)
    a_s[...] = jnp.zeros(a_s.shape, F32)

    def chunk(kstart, diag_offset):
        for h in range(hpb):
            kk = k_ref[pl.ds(kstart, tk), h * LANES:(h + 1) * LANES]
            vv = v_ref[pl.ds(kstart, tk), h * LANES:(h + 1) * LANES]
            for mp in range(2):
                i = 2 * h + mp
                s = lax.dot_general(qs[i], kk, (((1,), (1,)), ((), ())), preferred_element_type=F32)
                if diag_offset is not None:
                    r = lax.broadcasted_iota(jnp.int32, (tq, tk), 0)
                    c = lax.broadcasted_iota(jnp.int32, (tq, tk), 1) + diag_offset
                    s = jnp.where(c <= r, s, -jnp.inf)
                m_old = m_s[i]
                m_new = jnp.maximum(m_old, jnp.max(s, axis=1, keepdims=True))
                p = jnp.exp2(s - jnp.tile(m_new, (1, lt)))
                alpha = jnp.exp2(m_old - m_new)
                l_s[i] = alpha * l_s[i] + jnp.sum(p, axis=1, keepdims=True)
                a_s[i] = alpha * a_s[i] + jnp.dot(p.astype(BF16), vv, preferred_element_type=F32)
                m_s[i] = m_new

    per = tq // tk
    def body(i, carry):
        chunk(pl.multiple_of(i * tk, tk), None)
        return carry
    lax.fori_loop(0, qi * per, body, 0)
    for j in range(per):
        chunk(pl.multiple_of(qi * tq + j * tk, tk), j * tk)

    lam = (jnp.exp(jnp.sum(lq1[...] * lk1[...], axis=1, keepdims=True))
           - jnp.exp(jnp.sum(lq2[...] * lk2[...], axis=1, keepdims=True)) + lam_init)
    for h in range(hpb):
        out = a_s[2 * h] / l_s[2 * h] - lam * (a_s[2 * h + 1] / l_s[2 * h + 1])
        o_ref[:, h * LANES:(h + 1) * LANES] = (
            _rms(out, g_ref[...]) * (1.0 - lam_init)).astype(o_ref.dtype)


def _diff_attention(dq, dk, dv, lams, g_sub, bsz, seq, lam_init):
    m = bsz * seq
    tq = min(DIFF_TQ, seq)
    tk = min(DIFF_TK, tq)
    hpb = DIFF_HPB
    nq = seq // tq
    w = hpb * LANES
    qspec = pl.BlockSpec((tq, w), lambda b, h, i: (b * nq + i, h))
    kvspec = pl.BlockSpec((seq, w), lambda b, h, i: (b, h))
    vec = lambda n: pl.BlockSpec((1, n), lambda b, h, i: (0, 0))
    kern = functools.partial(_diff_kernel, tq=tq, tk=tk, hpb=hpb, lam_init=lam_init)
    return pl.pallas_call(
        kern,
        grid=(bsz, N_DIFF_HEADS // hpb, nq),
        in_specs=[qspec, kvspec, kvspec] + [vec(DIFF_QK_DIM)] * 4 + [vec(DIFF_V_DIM)],
        out_specs=qspec,
        out_shape=jax.ShapeDtypeStruct((m, DIFF_V), BF16),
        scratch_shapes=[pltpu.VMEM((2 * hpb, tq, LANES), F32)] * 3,
        compiler_params=_cparams(("parallel", "parallel", "arbitrary")),
        name="diff",
    )(dq, dk, dv, *lams, g_sub)


def _dsa_kernel(iq_ref, iw_ref, ik_ref, q_ref, k_ref, v_ref, o_ref,
                keys, hi16, lo16, sel16, cnt, halves, wrep, thr_s, m_s, l_s, acc,
                *, tqs, tqa, tk, k_top, seq):
    qi = pl.program_id(1)
    n_chunks = (qi * tqs + tqs + tk - 1) // tk
    lt = tk // LANES
    row = qi * tqs + lax.broadcasted_iota(jnp.int32, (tqs, tk), 0)
    col0 = lax.broadcasted_iota(jnp.int32, (tqs, tk), 1)
    lane = lax.broadcasted_iota(jnp.int32, (1, LANES), 1)

    zero = jnp.zeros((tqs, LANES), BF16)
    for j in range(N_IDX_HEADS // 2):
        pair = iq_ref[:, j * LANES:(j + 1) * LANES]
        halves[2 * j] = jnp.where(lane < IDX_DIM, pair, zero)
        halves[2 * j + 1] = jnp.where(lane >= IDX_DIM, pair, zero)
    iw = iw_ref[...]
    for h in range(N_IDX_HEADS):
        wrep[h] = jnp.broadcast_to(iw[:, h:h + 1], (tqs, LANES))

    def score_chunk(c, carry):
        kstart = pl.multiple_of(c * tk, tk)
        ik = ik_ref[pl.ds(kstart, tk), :]
        sc = jnp.zeros((tqs, tk), F32)
        for h in range(N_IDX_HEADS):
            lg = lax.dot_general(halves[h], ik, (((1,), (1,)), ((), ())),
                                 preferred_element_type=F32)
            sc = sc + jnp.tile(wrep[h], (1, lt)) * jnp.maximum(lg, 0.0)
        bits = lax.bitcast_convert_type(sc, jnp.int32)
        key = bits ^ ((bits >> 31) & jnp.int32(0x7FFFFFFF))
        key = jnp.where(col0 + kstart <= row, key, jnp.int32(INT_MIN))
        keys[:, pl.ds(kstart, tk)] = key
        hi16[:, pl.ds(kstart, tk)] = (key >> 16).astype(jnp.int16)
        lo16[:, pl.ds(kstart, tk)] = ((key & 0xFFFF) - 32768).astype(jnp.int16)
        return carry
    lax.fori_loop(0, n_chunks, score_chunk, 0)

    def count16(src, thr, strict=False):
        t16 = jnp.tile(jnp.broadcast_to(thr, (tqs, LANES)).astype(jnp.int16), (1, lt))
        cnt[...] = jnp.zeros(cnt.shape, jnp.int16)
        def body(c, carry):
            kstart = pl.multiple_of(c * tk, tk)
            blk = src[:, pl.ds(kstart, tk)]
            ok = (blk > t16) if strict else (blk >= t16)
            hit = jnp.where(ok, jnp.int16(1), jnp.int16(0))
            part = hit[:, 0:LANES]
            for g in range(1, lt):
                part = part + hit[:, g * LANES:(g + 1) * LANES]
            cnt[...] += part
            return carry
        lax.fori_loop(0, n_chunks, body, 0)
        return jnp.sum(cnt[...].astype(F32), axis=1, keepdims=True).astype(jnp.int32)

    def bisect16(src, need):
        def step(it, tau):
            cand = tau + jnp.left_shift(jnp.int32(1), 15 - it)
            return jnp.where(count16(src, cand) >= need, cand, tau)
        return lax.fori_loop(0, 16, step, jnp.full((tqs, 1), -32768, jnp.int32))

    tau_hi = bisect16(hi16, k_top)
    n_above = count16(hi16, tau_hi, strict=True)
    t16 = jnp.tile(jnp.broadcast_to(tau_hi, (tqs, LANES)).astype(jnp.int16), (1, lt))
    def low_chunk(c, carry):
        kstart = pl.multiple_of(c * tk, tk)
        same = hi16[:, pl.ds(kstart, tk)] == t16
        sel16[:, pl.ds(kstart, tk)] = jnp.where(same, lo16[:, pl.ds(kstart, tk)], jnp.int16(-32768))
        return carry
    lax.fori_loop(0, n_chunks, low_chunk, 0)
    tau_lo = bisect16(sel16, k_top - n_above)
    n_ge = n_above + count16(sel16, tau_lo)
    tau = tau_hi * 65536 + (tau_lo + 32768)

    live = tau > INT_MIN
    tied = jnp.logical_and(n_ge > k_top, live)
    any_tied = jnp.max(jnp.where(tied, 1, 0)) > 0

    @pl.when(any_tied)
    def _():
        def count_where(pred):
            def body(c, total):
                kstart = pl.multiple_of(c * tk, tk)
                hit = jnp.where(pred(keys[:, pl.ds(kstart, tk)], col0 + kstart), 1, 0)
                return total + jnp.sum(hit, axis=1, keepdims=True)
            return lax.fori_loop(0, n_chunks, body, jnp.zeros((tqs, 1), jnp.int32))

        n_gt = count_where(lambda kc, col: kc > tau)
        need = k_top - n_gt
        nbits = max(1, (seq - 1).bit_length())
        def idx_step(it, j):
            cand = j + jnp.left_shift(jnp.int32(1), nbits - 1 - it)
            below = count_where(lambda kc, col: jnp.logical_and(kc == tau, col < cand))
            return jnp.where(below < need, cand, j)
        jmax = lax.fori_loop(0, nbits, idx_step, jnp.zeros((tqs, 1), jnp.int32))

        def demote(c, carry):
            kstart = pl.multiple_of(c * tk, tk)
            kc = keys[:, pl.ds(kstart, tk)]
            drop = jnp.logical_and(jnp.logical_and(kc == tau, col0 + kstart > jmax), tied)
            keys[:, pl.ds(kstart, tk)] = jnp.where(drop, jnp.int32(INT_MIN), kc)
            return carry
        lax.fori_loop(0, n_chunks, demote, 0)

    thr_s[...] = jnp.broadcast_to(jnp.where(live, tau, jnp.int32(INT_MIN + 1)), thr_s.shape)

    nh = N_DSA_HEADS
    def attn_block(j, carry):
        r0 = pl.multiple_of(j * tqa, tqa)
        q_all = q_ref[:, pl.ds(r0, tqa), :].reshape(nh * tqa, DSA_HEAD_DIM)
        thr = jnp.tile(thr_s[pl.ds(r0, tqa), :], (1, lt))
        m_s[...] = jnp.full(m_s.shape, NEG_BIG, F32)
        l_s[...] = jnp.zeros(l_s.shape, F32)
        acc[...] = jnp.zeros(acc.shape, F32)

        def attn_chunk(c, carry2):
            kstart = pl.multiple_of(c * tk, tk)
            kk = k_ref[pl.ds(kstart, tk), :]
            vv = v_ref[pl.ds(kstart, tk), :]
            sel = keys[pl.ds(r0, tqa), pl.ds(kstart, tk)] >= thr
            s = lax.dot_general(q_all, kk, (((1,), (1,)), ((), ())), preferred_element_type=F32)
            s = jnp.where(sel[None], s.reshape(nh, tqa, tk), -jnp.inf).reshape(nh * tqa, tk)
            m_old = m_s[...]
            m_new = jnp.maximum(m_old, jnp.max(s, axis=1, keepdims=True))
            p = jnp.exp2(s - jnp.tile(m_new, (1, lt)))
            alpha = jnp.exp2(m_old - m_new)
            l_s[...] = alpha * l_s[...] + jnp.sum(p, axis=1, keepdims=True)
            acc[...] = alpha * acc[...] + jnp.dot(p.astype(BF16), vv, preferred_element_type=F32)
            m_s[...] = m_new
            return carry2
        lax.fori_loop(0, n_chunks, attn_chunk, 0)

        out = (acc[...] / l_s[...]).reshape(nh, tqa, DSA_HEAD_DIM)
        for h in range(nh):
            o_ref[pl.ds(r0, tqa), h * DSA_HEAD_DIM:(h + 1) * DSA_HEAD_DIM] = out[h].astype(o_ref.dtype)
        return carry
    lax.fori_loop(0, tqs // tqa, attn_block, 0)


def _dsa_attention(iq, iw, ik2, sq, sk, sv, bsz, seq, k_top):
    m = bsz * seq
    tqs = min(DSA_TQS, seq)
    tqa = min(DSA_TQA, tqs)
    tk = min(DSA_TK, seq)
    nq = seq // tqs
    rowspec = lambda n: pl.BlockSpec((tqs, n), lambda b, i: (b * nq + i, 0))
    seqspec = pl.BlockSpec((seq, LANES), lambda b, i: (b, 0))
    kern = functools.partial(_dsa_kernel, tqs=tqs, tqa=tqa, tk=tk, k_top=k_top, seq=seq)
    return pl.pallas_call(
        kern,
        grid=(bsz, nq),
        in_specs=[
            rowspec(IDX_Q), rowspec(LANES), seqspec,
            pl.BlockSpec((N_DSA_HEADS, tqs, DSA_HEAD_DIM), lambda b, i: (0, b * nq + i, 0)),
            seqspec, seqspec,
        ],
        out_specs=rowspec(DSA_Q),
        out_shape=jax.ShapeDtypeStruct((m, DSA_Q), BF16),
        scratch_shapes=[
            pltpu.VMEM((tqs, seq), jnp.int32),
            pltpu.VMEM((tqs, seq), jnp.int16),
            pltpu.VMEM((tqs, seq), jnp.int16),
            pltpu.VMEM((tqs, seq), jnp.int16),
            pltpu.VMEM((tqs, LANES), jnp.int16),
            pltpu.VMEM((N_IDX_HEADS, tqs, LANES), BF16),
            pltpu.VMEM((N_IDX_HEADS, tqs, LANES), F32),
            pltpu.VMEM((tqs, LANES), jnp.int32),
            pltpu.VMEM((N_DSA_HEADS * tqa, LANES), F32),
            pltpu.VMEM((N_DSA_HEADS * tqa, LANES), F32),
            pltpu.VMEM((N_DSA_HEADS * tqa, DSA_HEAD_DIM), F32),
        ],
        compiler_params=_cparams(("parallel", "arbitrary")),
        name="dsa",
    )(iq, iw, ik2, sq, sk, sv)


def _out_kernel(a_ref, b_ref, wo_ref, x_ref, mod_ref, gpost_ref, gpre_ref, x1_ref, h2_ref):
    ka = a_ref.shape[1]
    mixed = (jnp.dot(a_ref[...], wo_ref[0:ka, :], preferred_element_type=F32)
             + jnp.dot(b_ref[...], wo_ref[ka:, :], preferred_element_type=F32))
    x1 = x_ref[...] + mod_ref[0, 2:3, :] * _rms(mixed, gpost_ref[...])
    x1_ref[...] = x1
    h2 = _rms(x1, gpre_ref[...]) * (1.0 + mod_ref[0, 4:5, :]) + mod_ref[0, 3:4, :]
    h2_ref[...] = h2.astype(h2_ref.dtype)


def _out_proj(diff_out, dsa_out, wo, x2, mod3, g_post, g_pre, seq):
    m, d = x2.shape
    tm = min(ROW_TM, seq)
    tiles_per_batch = seq // tm
    row = lambda n: pl.BlockSpec((tm, n), lambda i: (i, 0))
    return pl.pallas_call(
        _out_kernel,
        grid=(m // tm,),
        in_specs=[
            row(diff_out.shape[1]), row(dsa_out.shape[1]),
            _resident(wo.shape, lambda i: (0, 0)),
            row(d),
            pl.BlockSpec((1, ADA_CHUNKS, d), lambda i: (i // tiles_per_batch, 0, 0)),
            _resident((1, d), lambda i: (0, 0)),
            _resident((1, d), lambda i: (0, 0)),
        ],
        out_specs=[row(d), row(d)],
        out_shape=[jax.ShapeDtypeStruct((m, d), F32), jax.ShapeDtypeStruct((m, d), BF16)],
        compiler_params=_cparams(("parallel",)),
        name="out",
    )(diff_out, dsa_out, wo, x2, mod3, g_post, g_pre)


def _ffn_kernel(h_ref, wg_ref, wu_ref, wd_ref, x1_ref, mod_ref, g_ref, o_ref, acc):
    f = pl.program_id(1)

    @pl.when(f == 0)
    def _():
        acc[...] = jnp.zeros(acc.shape, F32)

    h = h_ref[...]
    gate = jnp.dot(h, wg_ref[...], preferred_element_type=F32)
    up = jnp.dot(h, wu_ref[...], preferred_element_type=F32)
    act = (gate * jax.nn.sigmoid(gate) * up).astype(BF16)
    acc[...] += jnp.dot(act, wd_ref[...], preferred_element_type=F32)

    @pl.when(f == pl.num_programs(1) - 1)
    def _():
        o_ref[...] = x1_ref[...] + mod_ref[0, 5:6, :] * _rms(acc[...], g_ref[...])


def _ffn(h2, wg, wu, wd, x1, mod3, g_post, seq):
    m, d = x1.shape
    dff = wg.shape[1]
    tm = min(ROW_TM, seq)
    tf = FFN_TF
    tiles_per_batch = seq // tm
    row = pl.BlockSpec((tm, d), lambda i, f: (i, 0))
    return pl.pallas_call(
        _ffn_kernel,
        grid=(m // tm, dff // tf),
        in_specs=[
            row,
            pl.BlockSpec((d, tf), lambda i, f: (0, f)),
            pl.BlockSpec((d, tf), lambda i, f: (0, f)),
            pl.BlockSpec((tf, d), lambda i, f: (f, 0)),
            row,
            pl.BlockSpec((1, ADA_CHUNKS, d), lambda i, f: (i // tiles_per_batch, 0, 0)),
            _resident((1, d), lambda i, f: (0, 0)),
        ],
        out_specs=row,
        out_shape=jax.ShapeDtypeStruct((m, d), F32),
        scratch_shapes=[pltpu.VMEM((tm, d), F32)],
        compiler_params=_cparams(("parallel", "arbitrary")),
        name="ffn",
    )(h2, wg, wu, wd, x1, mod3, g_post)


def _split_w_in(w_in):
    offs = [0]
    for sz in PROJ_SIZES:
        offs.append(offs[-1] + sz)
    p = [w_in[:, offs[i]:offs[i + 1]] for i in range(len(PROJ_SIZES))]
    d = w_in.shape[0]
    wik2 = jnp.concatenate([p[7], p[7]], axis=1)
    wiw = jnp.concatenate([p[8], jnp.zeros((d, LANES - IDX_W), w_in.dtype)], axis=1)
    parts = p[:7] + [wik2, wiw]
    return [w.astype(BF16) for w in parts]


def kernel(x, c, positions, w_ada, b_ada, g_attn_pre, g_attn_post, g_ffn_pre, g_ffn_post, w_in, lambda_q1, lambda_k1, lambda_q2, lambda_k2, g_diff_sub, w_o, w_gate, w_up, w_down):
    bsz, seq, d = x.shape
    depth = w_ada.shape[0]
    k_top = min(INDEX_TOPK, seq // 4)
    m = bsz * seq
    tabs = _rope_tables(positions)
    x2 = x.reshape(m, d)
    for l in range(depth):
        lam_init = 0.8 - 0.6 * math.exp(-0.3 * l)
        mod3 = _ada(c, w_ada[l], b_ada[l]).reshape(bsz, ADA_CHUNKS, d)
        vec = lambda a: a[l].reshape(1, -1)
        odq, odk, odv, osq, osk, osv, oiq, oik, oiw = _proj(
            x2, vec(g_attn_pre), mod3, tabs, _split_w_in(w_in[l]), seq)
        lams = [vec(lambda_q1), vec(lambda_k1), vec(lambda_q2), vec(lambda_k2)]
        diff_out = _diff_attention(odq, odk, odv, lams, vec(g_diff_sub), bsz, seq, lam_init)
        dsa_out = _dsa_attention(oiq, oiw, oik, osq, osk, osv, bsz, seq, k_top)
        x1, h2 = _out_proj(diff_out, dsa_out, w_o[l].astype(BF16), x2, mod3,
                           vec(g_attn_post), vec(g_ffn_pre), seq)
        x2 = _ffn(h2, w_gate[l].astype(BF16), w_up[l].astype(BF16), w_down[l].astype(BF16),
                  x1, mod3, vec(g_ffn_post), seq)
    return x2.reshape(bsz, seq, d)
```

```python
import functools
import math

import jax
import jax.numpy as jnp
from jax import lax
from jax.experimental import pallas as pl
from jax.experimental.pallas import tpu as pltpu

N_DIFF_HEADS = 8
DIFF_QK_DIM = 64
DIFF_V_DIM = 128
N_DSA_HEADS = 8
DSA_HEAD_DIM = 128
N_IDX_HEADS = 16
IDX_DIM = 64
INDEX_TOPK = 256
ROPE_THETA = 500000.0
ROPE_FRACTION = 4
ADA_CHUNKS = 6
RMS_EPS = 1e-6

DIFF_Q = N_DIFF_HEADS * 2 * DIFF_QK_DIM
DIFF_K = DIFF_Q
DIFF_V = N_DIFF_HEADS * DIFF_V_DIM
DSA_Q = N_DSA_HEADS * DSA_HEAD_DIM
DSA_K = DSA_HEAD_DIM
DSA_V = DSA_HEAD_DIM
IDX_Q = N_IDX_HEADS * IDX_DIM
IDX_K = IDX_DIM
IDX_W = N_IDX_HEADS
PROJ_SIZES = (DIFF_Q, DIFF_K, DIFF_V, DSA_Q, DSA_K, DSA_V, IDX_Q, IDX_K, IDX_W)

LANES = 128
INT16_TILE_ROWS = 16
MXU_WIDTH = 256
VMEM_LIMIT_BYTES = 56 * 1024 * 1024

ADA_TK, ADA_TN = 1024, 2048
ROPE_TM = 1024
ROW_TM = 512
OUT_SUB = 2
FFN_TF = 512
DIFF_TQ = 1024
DIFF_TK = 1024
DIFF_DIAG_TK = 256
DIFF_HPB = 2
DSA_TQS = 512
DSA_TQA = 256
DSA_TK = 512

F32 = jnp.float32
BF16 = jnp.bfloat16
INT_MIN = -(2 ** 31)
NEG_BIG = -1e30
LOG2E = math.log2(math.e)


def _cparams(sem):
    return pltpu.CompilerParams(dimension_semantics=sem, vmem_limit_bytes=VMEM_LIMIT_BYTES)


def _resident(shape, index_map):
    return pl.BlockSpec(shape, index_map, pipeline_mode=pl.Buffered(1))


def _rms(x, g):
    ms = jnp.mean(x * x, axis=-1, keepdims=True)
    return x * lax.rsqrt(ms + RMS_EPS) * g


def _ada_kernel(ct_ref, w_ref, b_ref, o_ref):
    k = pl.program_id(1)

    @pl.when(k == 0)
    def _():
        o_ref[...] = jnp.broadcast_to(b_ref[...], o_ref.shape)

    cv = ct_ref[...]
    s = cv * jax.nn.sigmoid(cv)
    w = w_ref[...]
    for b in range(o_ref.shape[0]):
        o_ref[b:b + 1, :] += jnp.sum(w * s[:, b:b + 1], axis=0, keepdims=True)


def _ada(c, w_ada, b_ada):
    bsz, d = c.shape
    n = w_ada.shape[1]
    tk, tn = ADA_TK, ADA_TN
    return pl.pallas_call(
        _ada_kernel,
        grid=(n // tn, d // tk),
        in_specs=[
            pl.BlockSpec((tk, bsz), lambda j, k: (k, 0)),
            pl.BlockSpec((tk, tn), lambda j, k: (k, j)),
            pl.BlockSpec((1, tn), lambda j, k: (0, j)),
        ],
        out_specs=pl.BlockSpec((bsz, tn), lambda j, k: (0, j)),
        out_shape=jax.ShapeDtypeStruct((bsz, n), F32),
        compiler_params=_cparams(("parallel", "arbitrary")),
        name="ada",
    )(c.T, w_ada, b_ada.reshape(1, n))


def _rope_kernel(pos_ref, c64_ref, s64_ref, c128_ref, s128_ref):
    pos = pos_ref[...]
    lane = lax.broadcasted_iota(jnp.int32, (1, LANES), 1)
    r64, r128 = 64 // ROPE_FRACTION, 128 // ROPE_FRACTION
    in64 = lane < r64
    in128 = jnp.logical_and(lane >= r64, lane < r64 + r128)
    k = jnp.where(in64, lane % (r64 // 2), (lane - r64) % (r128 // 2)).astype(F32)
    rot = jnp.where(in64, float(r64), float(r128))
    inv = jnp.power(jnp.float32(ROPE_THETA), -(k * 2.0 / rot))
    inv = jnp.where(jnp.logical_or(in64, in128), inv, 0.0)
    ang = pos * inv
    cos, sin = jnp.cos(ang), jnp.sin(ang)
    second = jnp.logical_and(lane >= 64, lane < 64 + r64)
    sign64 = jnp.where(lane % 64 < r64 // 2, -1.0, 1.0)
    c64_ref[...] = jnp.where(in64, cos, jnp.where(second, pltpu.roll(cos, 64, axis=1), 1.0))
    s64_ref[...] = jnp.where(in64, sin, jnp.where(second, pltpu.roll(sin, 64, axis=1), 0.0)) * sign64
    first = lane < r128
    sign128 = jnp.where(lane < r128 // 2, -1.0, 1.0)
    c128_ref[...] = jnp.where(first, pltpu.roll(cos, LANES - r64, axis=1), 1.0)
    s128_ref[...] = jnp.where(first, pltpu.roll(sin, LANES - r64, axis=1), 0.0) * sign128


def _rope_tables(positions):
    bsz, s = positions.shape
    m = bsz * s
    posb = jnp.broadcast_to(positions.reshape(m, 1).astype(F32), (m, LANES))
    tm = min(ROPE_TM, m)
    spec = pl.BlockSpec((tm, LANES), lambda i: (i, 0))
    shp = jax.ShapeDtypeStruct((m, LANES), F32)
    return pl.pallas_call(
        _rope_kernel,
        grid=(m // tm,),
        in_specs=[spec],
        out_specs=[spec] * 4,
        out_shape=[shp] * 4,
        compiler_params=_cparams(("parallel",)),
        name="rope",
    )(posb)


def _rope_apply(y, c_tab, s_tab, hd):
    n = y.shape[1]
    reps = n // LANES
    half = hd // ROPE_FRACTION // 2
    lane = lax.broadcasted_iota(jnp.int32, (1, n), 1) % hd
    cc = jnp.tile(c_tab, (1, reps)) if reps > 1 else c_tab
    ss = jnp.tile(s_tab, (1, reps)) if reps > 1 else s_tab
    ahead = pltpu.roll(y, n - half, axis=1)
    behind = pltpu.roll(y, half, axis=1)
    partner = jnp.where(lane < half, ahead, behind)
    return y * cc + partner * ss


def _proj_kernel(x_ref, g_ref, mod_ref, c64_ref, s64_ref, c128_ref, s128_ref, wmain, wtail,
                 odq, odk, odv, osq, osk, osv, oiq, oik, oiw):
    x = x_ref[...]
    h = _rms(x, g_ref[...]) * (1.0 + mod_ref[0, 1:2, :]) + mod_ref[0, 0:1, :]
    hb = h.astype(BF16)
    c64, s64 = c64_ref[...], s64_ref[...]
    c128, s128 = c128_ref[...], s128_ref[...]
    rope64 = lambda y: _rope_apply(y, c64, s64, 64)
    rope128 = lambda y: _rope_apply(y, c128, s128, 128)
    diff_scale = DIFF_QK_DIM ** -0.5 * LOG2E
    dsa_scale = DSA_HEAD_DIM ** -0.5 * LOG2E
    idx_w_scale = (N_IDX_HEADS ** -0.5) * (IDX_DIM ** -0.5)

    def tiles(w_ref, start, n):
        for c0 in range(0, n, MXU_WIDTH):
            y = jnp.dot(hb, w_ref[:, start + c0:start + c0 + MXU_WIDTH], preferred_element_type=F32)
            yield c0, y

    offs = [0]
    for sz in PROJ_SIZES:
        offs.append(offs[-1] + sz)
    for c0, y in tiles(wmain, offs[0], DIFF_Q):
        odq[:, c0:c0 + MXU_WIDTH] = (rope64(y) * diff_scale).astype(odq.dtype)
    for c0, y in tiles(wmain, offs[1], DIFF_K):
        odk[:, c0:c0 + MXU_WIDTH] = rope64(y).astype(odk.dtype)
    for c0, y in tiles(wmain, offs[2], DIFF_V):
        odv[:, c0:c0 + MXU_WIDTH] = y.astype(odv.dtype)
    for c0, y in tiles(wmain, offs[3], DSA_Q):
        y = rope128(y) * dsa_scale
        for j in range(MXU_WIDTH // DSA_HEAD_DIM):
            osq[c0 // DSA_HEAD_DIM + j] = y[:, j * DSA_HEAD_DIM:(j + 1) * DSA_HEAD_DIM].astype(osq.dtype)
    for c0, y in tiles(wmain, offs[4], DSA_K + DSA_V):
        osk[...] = rope128(y[:, :DSA_K]).astype(osk.dtype)
        osv[...] = y[:, DSA_K:].astype(osv.dtype)
    for c0, y in tiles(wmain, offs[6], IDX_Q):
        oiq[:, c0:c0 + MXU_WIDTH] = rope64(y).astype(oiq.dtype)
    for c0, y in tiles(wtail, 0, 2 * LANES):
        oik[...] = rope64(y[:, :LANES]).astype(oik.dtype)
        oiw[...] = y[:, LANES:] * idx_w_scale


def _proj(x2, g, mod3, tabs, wparts, seq):
    m, d = x2.shape
    tm = min(ROW_TM, seq)
    tiles_per_batch = seq // tm
    row = lambda n: pl.BlockSpec((tm, n), lambda i: (i, 0))
    in_specs = [
        row(d),
        _resident((1, d), lambda i: (0, 0)),
        pl.BlockSpec((1, ADA_CHUNKS, d), lambda i: (i // tiles_per_batch, 0, 0)),
        row(LANES), row(LANES), row(LANES), row(LANES),
    ] + [_resident(w.shape, lambda i: (0, 0)) for w in wparts]
    out_specs = [
        row(DIFF_Q), row(DIFF_K), row(DIFF_V),
        pl.BlockSpec((N_DSA_HEADS, tm, DSA_HEAD_DIM), lambda i: (0, i, 0)),
        row(LANES), row(LANES), row(IDX_Q), row(LANES), row(LANES),
    ]
    sds = jax.ShapeDtypeStruct
    out_shape = [
        sds((m, DIFF_Q), BF16), sds((m, DIFF_K), BF16), sds((m, DIFF_V), BF16),
        sds((N_DSA_HEADS, m, DSA_HEAD_DIM), BF16),
        sds((m, LANES), BF16), sds((m, LANES), BF16), sds((m, IDX_Q), BF16),
        sds((m, LANES), BF16), sds((m, LANES), F32),
    ]
    return pl.pallas_call(
        _proj_kernel,
        grid=(m // tm,),
        in_specs=in_specs,
        out_specs=out_specs,
        out_shape=out_shape,
        compiler_params=_cparams(("parallel",)),
        name="proj",
    )(x2, g, mod3, *tabs, *wparts)


def _diff_kernel(q_ref, k_ref, v_ref, lq1, lk1, lq2, lk2, g_ref, o_ref,
                 m_s, l_s, a_s, *, tq, tk, hpb, lam_init):
    qi = pl.program_id(2)
    lane = lax.broadcasted_iota(jnp.int32, (1, LANES), 1)
    qs = []
    for h in range(hpb):
        q = q_ref[:, h * LANES:(h + 1) * LANES]
        zero = jnp.zeros_like(q)
        qs.append(jnp.where(lane < DIFF_QK_DIM, q, zero))
        qs.append(jnp.where(lane >= DIFF_QK_DIM, q, zero))
    m_s[...] = jnp.full(m_s.shape, NEG_BIG, F32)
    l_s[...] = jnp.zeros(l_s.shape, F32)
    a_s[...] = jnp.zeros(a_s.shape, F32)

    def chunk(kstart, kw, r0, diag_offset):
        nr = tq - r0
        for h in range(hpb):
            kk = k_ref[pl.ds(kstart, kw), h * LANES:(h + 1) * LANES]
            vv = v_ref[pl.ds(kstart, kw), h * LANES:(h + 1) * LANES]
            for mp in range(2):
                i = 2 * h + mp
                s = lax.dot_general(qs[i][r0:, :], kk, (((1,), (1,)), ((), ())),
                                    preferred_element_type=F32)
                if diag_offset is not None:
                    r = lax.broadcasted_iota(jnp.int32, (nr, kw), 0) + r0
                    c = lax.broadcasted_iota(jnp.int32, (nr, kw), 1) + diag_offset
                    s = jnp.where(c <= r, s, -jnp.inf)
                m_old = m_s[i, r0:, :]
                m_new = jnp.maximum(m_old, jnp.max(s, axis=1, keepdims=True))
                p = jnp.exp2(s - jnp.tile(m_new, (1, kw // LANES)))
                alpha = jnp.exp2(m_old - m_new)
                l_s[i, r0:, :] = alpha * l_s[i, r0:, :] + jnp.sum(p, axis=1, keepdims=True)
                a_s[i, r0:, :] = alpha * a_s[i, r0:, :] + jnp.dot(
                    p.astype(BF16), vv, preferred_element_type=F32)
                m_s[i, r0:, :] = m_new

    per = tq // tk
    def body(i, carry):
        chunk(pl.multiple_of(i * tk, tk), tk, 0, None)
        return carry
    lax.fori_loop(0, qi * per, body, 0)
    dk = min(DIFF_DIAG_TK, tk)
    for j in range(tq // dk):
        chunk(pl.multiple_of(qi * tq + j * dk, dk), dk, j * dk, j * dk)

    lam = (jnp.exp(jnp.sum(lq1[...] * lk1[...], axis=1, keepdims=True))
           - jnp.exp(jnp.sum(lq2[...] * lk2[...], axis=1, keepdims=True)) + lam_init)
    for h in range(hpb):
        out = a_s[2 * h] / l_s[2 * h] - lam * (a_s[2 * h + 1] / l_s[2 * h + 1])
        o_ref[:, h * LANES:(h + 1) * LANES] = (
            _rms(out, g_ref[...]) * (1.0 - lam_init)).astype(o_ref.dtype)


def _diff_attention(dq, dk, dv, lams, g_sub, bsz, seq, lam_init):
    m = bsz * seq
    tq = min(DIFF_TQ, seq)
    tk = min(DIFF_TK, tq)
    hpb = DIFF_HPB
    nq = seq // tq
    w = hpb * LANES
    qspec = pl.BlockSpec((tq, w), lambda b, h, i: (b * nq + i, h))
    kvspec = pl.BlockSpec((seq, w), lambda b, h, i: (b, h))
    vec = lambda n: pl.BlockSpec((1, n), lambda b, h, i: (0, 0))
    kern = functools.partial(_diff_kernel, tq=tq, tk=tk, hpb=hpb, lam_init=lam_init)
    return pl.pallas_call(
        kern,
        grid=(bsz, N_DIFF_HEADS // hpb, nq),
        in_specs=[qspec, kvspec, kvspec] + [vec(DIFF_QK_DIM)] * 4 + [vec(DIFF_V_DIM)],
        out_specs=qspec,
        out_shape=jax.ShapeDtypeStruct((m, DIFF_V), BF16),
        scratch_shapes=[pltpu.VMEM((2 * hpb, tq, LANES), F32)] * 3,
        compiler_params=_cparams(("parallel", "parallel", "arbitrary")),
        name="diff",
    )(dq, dk, dv, *lams, g_sub)


def _dsa_kernel(iq_ref, iw_ref, ik_ref, q_ref, k_ref, v_ref, o_ref,
                keys, hi_t, lo_t, sel_t, halves, wrep, thr_s, m_s, l_s, acc,
                *, tqs, tqa, tk, k_top, seq):
    qi = pl.program_id(1)
    n_chunks = (qi * tqs + tqs + tk - 1) // tk
    lt = tk // LANES
    pk = INT16_TILE_ROWS
    row = qi * tqs + lax.broadcasted_iota(jnp.int32, (tqs, tk), 0)
    col0 = lax.broadcasted_iota(jnp.int32, (tqs, tk), 1)
    lane = lax.broadcasted_iota(jnp.int32, (1, LANES), 1)

    zero = jnp.zeros((tqs, LANES), BF16)
    for j in range(N_IDX_HEADS // 2):
        pair = iq_ref[:, j * LANES:(j + 1) * LANES]
        halves[2 * j] = jnp.where(lane < IDX_DIM, pair, zero)
        halves[2 * j + 1] = jnp.where(lane >= IDX_DIM, pair, zero)
    iw = iw_ref[...]
    for h in range(N_IDX_HEADS):
        wrep[h] = jnp.broadcast_to(iw[:, h:h + 1], (tqs, LANES))

    def score_chunk(c, carry):
        kstart = pl.multiple_of(c * tk, tk)
        ik = ik_ref[pl.ds(kstart, tk), :]
        sc = jnp.zeros((tqs, tk), F32)
        for h in range(N_IDX_HEADS):
            lg = lax.dot_general(halves[h], ik, (((1,), (1,)), ((), ())),
                                 preferred_element_type=F32)
            sc = sc + jnp.tile(wrep[h], (1, lt)) * jnp.maximum(lg, 0.0)
        bits = lax.bitcast_convert_type(sc, jnp.int32)
        key = bits ^ ((bits >> 31) & jnp.int32(0x7FFFFFFF))
        key = jnp.where(col0 + kstart <= row, key, jnp.int32(INT_MIN))
        keys[:, pl.ds(kstart, tk)] = key
        key_t = key.T
        hi_t[pl.ds(kstart, tk), :] = (key_t >> 16).astype(jnp.int16)
        lo_t[pl.ds(kstart, tk), :] = ((key_t & 0xFFFF) - 32768).astype(jnp.int16)
        return carry
    lax.fori_loop(0, n_chunks, score_chunk, 0)

    def count16(src, thr, strict=False):
        t16 = jnp.broadcast_to(thr, (pk, tqs)).astype(jnp.int16)
        def body(c, cnt):
            kstart = pl.multiple_of(c * tk, tk)
            blk = src[pl.ds(kstart, tk), :].reshape(tk // pk, pk, tqs)
            ok = (blk > t16[None]) if strict else (blk >= t16[None])
            hit = jnp.where(ok, jnp.int16(1), jnp.int16(0))
            part = hit[0]
            for g in range(1, tk // pk):
                part = part + hit[g]
            return cnt + part
        cnt = lax.fori_loop(0, n_chunks, body, jnp.zeros((pk, tqs), jnp.int16))
        return jnp.sum(cnt.astype(F32), axis=0, keepdims=True).astype(jnp.int32)

    def bisect16(src, need):
        def step(it, tau):
            cand = tau + jnp.left_shift(jnp.int32(1), 15 - it)
            return jnp.where(count16(src, cand) >= need, cand, tau)
        return lax.fori_loop(0, 16, step, jnp.full((1, tqs), -32768, jnp.int32))

    tau_hi = bisect16(hi_t, k_top)
    n_above = count16(hi_t, tau_hi, strict=True)
    t16 = jnp.broadcast_to(tau_hi, (pk, tqs)).astype(jnp.int16)
    def low_chunk(c, carry):
        kstart = pl.multiple_of(c * tk, tk)
        same = hi_t[pl.ds(kstart, tk), :].reshape(tk // pk, pk, tqs) == t16[None]
        lo = lo_t[pl.ds(kstart, tk), :].reshape(tk // pk, pk, tqs)
        sel_t[pl.ds(kstart, tk), :] = jnp.where(same, lo, jnp.int16(-32768)).reshape(tk, tqs)
        return carry
    lax.fori_loop(0, n_chunks, low_chunk, 0)
    tau_lo = bisect16(sel_t, k_top - n_above)
    n_ge = n_above + count16(sel_t, tau_lo)
    tau_row = tau_hi * 65536 + (tau_lo + 32768)
    tied_row = jnp.logical_and(n_ge > k_top, tau_row > INT_MIN)
    any_tied = jnp.max(jnp.where(tied_row, 1, 0)) > 0
    thr_s[...] = jnp.broadcast_to(tau_row, (LANES, tqs)).T

    @pl.when(any_tied)
    def _():
        tau = thr_s[:, 0:1]
        def count_where(pred):
            def body(c, total):
                kstart = pl.multiple_of(c * tk, tk)
                hit = jnp.where(pred(keys[:, pl.ds(kstart, tk)], col0 + kstart), 1, 0)
                return total + jnp.sum(hit, axis=1, keepdims=True)
            return lax.fori_loop(0, n_chunks, body, jnp.zeros((tqs, 1), jnp.int32))

        n_gt = count_where(lambda kc, col: kc > tau)
        n_all = count_where(lambda kc, col: kc >= tau)
        tied = jnp.logical_and(n_all > k_top, tau > INT_MIN)
        need = k_top - n_gt
        nbits = max(1, (seq - 1).bit_length())
        def idx_step(it, j):
            cand = j + jnp.left_shift(jnp.int32(1), nbits - 1 - it)
            below = count_where(lambda kc, col: jnp.logical_and(kc == tau, col < cand))
            return jnp.where(below < need, cand, j)
        jmax = lax.fori_loop(0, nbits, idx_step, jnp.zeros((tqs, 1), jnp.int32))

        def demote(c, carry):
            kstart = pl.multiple_of(c * tk, tk)
            kc = keys[:, pl.ds(kstart, tk)]
            drop = jnp.logical_and(jnp.logical_and(kc == tau, col0 + kstart > jmax), tied)
            keys[:, pl.ds(kstart, tk)] = jnp.where(drop, jnp.int32(INT_MIN), kc)
            return carry
        lax.fori_loop(0, n_chunks, demote, 0)

    thr_s[...] = jnp.maximum(thr_s[...], jnp.int32(INT_MIN + 1))

    nh = N_DSA_HEADS
    def attn_block(j, carry):
        r0 = pl.multiple_of(j * tqa, tqa)
        q_all = q_ref[:, pl.ds(r0, tqa), :].reshape(nh * tqa, DSA_HEAD_DIM)
        thr = jnp.tile(thr_s[pl.ds(r0, tqa), :], (1, lt))
        m_s[...] = jnp.full(m_s.shape, NEG_BIG, F32)
        l_s[...] = jnp.zeros(l_s.shape, F32)
        acc[...] = jnp.zeros(acc.shape, F32)

        def attn_chunk(c, carry2):
            kstart = pl.multiple_of(c * tk, tk)
            kk = k_ref[pl.ds(kstart, tk), :]
            vv = v_ref[pl.ds(kstart, tk), :]
            sel = keys[pl.ds(r0, tqa), pl.ds(kstart, tk)] >= thr
            bias = jnp.where(sel, 0.0, -jnp.inf)
            s = lax.dot_general(q_all, kk, (((1,), (1,)), ((), ())), preferred_element_type=F32)
            s = (s.reshape(nh, tqa, tk) + bias[None]).reshape(nh * tqa, tk)
            m_old = m_s[...]
            m_new = jnp.maximum(m_old, jnp.max(s, axis=1, keepdims=True))
            p = jnp.exp2(s - jnp.tile(m_new, (1, lt)))
            alpha = jnp.exp2(m_old - m_new)
            l_s[...] = alpha * l_s[...] + jnp.sum(p, axis=1, keepdims=True)
            acc[...] = alpha * acc[...] + jnp.dot(p.astype(BF16), vv, preferred_element_type=F32)
            m_s[...] = m_new
            return carry2
        lax.fori_loop(0, n_chunks, attn_chunk, 0)

        out = (acc[...] / l_s[...]).reshape(nh, tqa, DSA_HEAD_DIM)
        for h in range(nh):
            o_ref[pl.ds(r0, tqa), h * DSA_HEAD_DIM:(h + 1) * DSA_HEAD_DIM] = out[h].astype(o_ref.dtype)
        return carry
    lax.fori_loop(0, tqs // tqa, attn_block, 0)


def _dsa_attention(iq, iw, ik2, sq, sk, sv, bsz, seq, k_top):
    m = bsz * seq
    tqs = min(DSA_TQS, seq)
    tqa = min(DSA_TQA, tqs)
    tk = min(DSA_TK, seq)
    nq = seq // tqs
    rowspec = lambda n: pl.BlockSpec((tqs, n), lambda b, i: (b * nq + i, 0))
    seqspec = pl.BlockSpec((seq, LANES), lambda b, i: (b, 0))
    kern = functools.partial(_dsa_kernel, tqs=tqs, tqa=tqa, tk=tk, k_top=k_top, seq=seq)
    return pl.pallas_call(
        kern,
        grid=(bsz, nq),
        in_specs=[
            rowspec(IDX_Q), rowspec(LANES), seqspec,
            pl.BlockSpec((N_DSA_HEADS, tqs, DSA_HEAD_DIM), lambda b, i: (0, b * nq + i, 0)),
            seqspec, seqspec,
        ],
        out_specs=rowspec(DSA_Q),
        out_shape=jax.ShapeDtypeStruct((m, DSA_Q), BF16),
        scratch_shapes=[
            pltpu.VMEM((tqs, seq), jnp.int32),
            pltpu.VMEM((seq, tqs), jnp.int16),
            pltpu.VMEM((seq, tqs), jnp.int16),
            pltpu.VMEM((seq, tqs), jnp.int16),
            pltpu.VMEM((N_IDX_HEADS, tqs, LANES), BF16),
            pltpu.VMEM((N_IDX_HEADS, tqs, LANES), F32),
            pltpu.VMEM((tqs, LANES), jnp.int32),
            pltpu.VMEM((N_DSA_HEADS * tqa, LANES), F32),
            pltpu.VMEM((N_DSA_HEADS * tqa, LANES), F32),
            pltpu.VMEM((N_DSA_HEADS * tqa, DSA_HEAD_DIM), F32),
        ],
        compiler_params=_cparams(("parallel", "arbitrary")),
        name="dsa",
    )(iq, iw, ik2, sq, sk, sv)


def _out_kernel(a_ref, b_ref, wo_ref, x_ref, mod_ref, gpost_ref, gpre_ref, x1_ref, h2_ref):
    ka = a_ref.shape[1]
    ts = a_ref.shape[0] // OUT_SUB
    for r in range(OUT_SUB):
        rows = slice(r * ts, (r + 1) * ts)
        mixed = (jnp.dot(a_ref[rows, :], wo_ref[0:ka, :], preferred_element_type=F32)
                 + jnp.dot(b_ref[rows, :], wo_ref[ka:, :], preferred_element_type=F32))
        x1 = x_ref[rows, :] + mod_ref[0, 2:3, :] * _rms(mixed, gpost_ref[...])
        x1_ref[rows, :] = x1
        h2 = _rms(x1, gpre_ref[...]) * (1.0 + mod_ref[0, 4:5, :]) + mod_ref[0, 3:4, :]
        h2_ref[rows, :] = h2.astype(h2_ref.dtype)


def _out_proj(diff_out, dsa_out, wo, x2, mod3, g_post, g_pre, seq):
    m, d = x2.shape
    tm = min(ROW_TM, seq)
    tiles_per_batch = seq // tm
    row = lambda n: pl.BlockSpec((tm, n), lambda i: (i, 0))
    return pl.pallas_call(
        _out_kernel,
        grid=(m // tm,),
        in_specs=[
            row(diff_out.shape[1]), row(dsa_out.shape[1]),
            _resident(wo.shape, lambda i: (0, 0)),
            row(d),
            pl.BlockSpec((1, ADA_CHUNKS, d), lambda i: (i // tiles_per_batch, 0, 0)),
            _resident((1, d), lambda i: (0, 0)),
            _resident((1, d), lambda i: (0, 0)),
        ],
        out_specs=[row(d), row(d)],
        out_shape=[jax.ShapeDtypeStruct((m, d), F32), jax.ShapeDtypeStruct((m, d), BF16)],
        compiler_params=_cparams(("parallel",)),
        name="out",
    )(diff_out, dsa_out, wo, x2, mod3, g_post, g_pre)


def _ffn_kernel(h_ref, wg_ref, wu_ref, wd_ref, x1_ref, mod_ref, g_ref, o_ref, acc):
    f = pl.program_id(1)

    @pl.when(f == 0)
    def _():
        acc[...] = jnp.zeros(acc.shape, F32)

    h = h_ref[...]
    gate = jnp.dot(h, wg_ref[...], preferred_element_type=F32)
    up = jnp.dot(h, wu_ref[...], preferred_element_type=F32)
    act = (gate * jax.nn.sigmoid(gate) * up).astype(BF16)
    acc[...] += jnp.dot(act, wd_ref[...], preferred_element_type=F32)

    @pl.when(f == pl.num_programs(1) - 1)
    def _():
        o_ref[...] = x1_ref[...] + mod_ref[0, 5:6, :] * _rms(acc[...], g_ref[...])


def _ffn(h2, wg, wu, wd, x1, mod3, g_post, seq):
    m, d = x1.shape
    dff = wg.shape[1]
    tm = min(ROW_TM, seq)
    tf = FFN_TF
    tiles_per_batch = seq // tm
    row = pl.BlockSpec((tm, d), lambda i, f: (i, 0))
    return pl.pallas_call(
        _ffn_kernel,
        grid=(m // tm, dff // tf),
        in_specs=[
            row,
            pl.BlockSpec((d, tf), lambda i, f: (0, f)),
            pl.BlockSpec((d, tf), lambda i, f: (0, f)),
            pl.BlockSpec((tf, d), lambda i, f: (f, 0)),
            row,
            pl.BlockSpec((1, ADA_CHUNKS, d), lambda i, f: (i // tiles_per_batch, 0, 0)),
            _resident((1, d), lambda i, f: (0, 0)),
        ],
        out_specs=row,
        out_shape=jax.ShapeDtypeStruct((m, d), F32),
        scratch_shapes=[pltpu.VMEM((tm, d), F32)],
        compiler_params=_cparams(("parallel", "arbitrary")),
        name="ffn",
    )(h2, wg, wu, wd, x1, mod3, g_post)


def _split_w_in(w_in):
    offs = [0]
    for sz in PROJ_SIZES:
        offs.append(offs[-1] + sz)
    d = w_in.shape[0]
    w_main = w_in[:, :offs[7]].astype(BF16)
    w_ik = w_in[:, offs[7]:offs[8]]
    w_iw = w_in[:, offs[8]:offs[9]]
    w_tail = jnp.concatenate(
        [w_ik, w_ik, w_iw, jnp.zeros((d, LANES - IDX_W), w_in.dtype)], axis=1).astype(BF16)
    return [w_main, w_tail]


def kernel(x, c, positions, w_ada, b_ada, g_attn_pre, g_attn_post, g_ffn_pre, g_ffn_post, w_in, lambda_q1, lambda_k1, lambda_q2, lambda_k2, g_diff_sub, w_o, w_gate, w_up, w_down):
    bsz, seq, d = x.shape
    depth = w_ada.shape[0]
    k_top = min(INDEX_TOPK, seq // 4)
    m = bsz * seq
    tabs = _rope_tables(positions)
    x2 = x.reshape(m, d)
    for l in range(depth):
        lam_init = 0.8 - 0.6 * math.exp(-0.3 * l)
        mod3 = _ada(c, w_ada[l], b_ada[l]).reshape(bsz, ADA_CHUNKS, d)
        vec = lambda a: a[l].reshape(1, -1)
        odq, odk, odv, osq, osk, osv, oiq, oik, oiw = _proj(
            x2, vec(g_attn_pre), mod3, tabs, _split_w_in(w_in[l]), seq)
        lams = [vec(lambda_q1), vec(lambda_k1), vec(lambda_q2), vec(lambda_k2)]
        diff_out = _diff_attention(odq, odk, odv, lams, vec(g_diff_sub), bsz, seq, lam_init)
        dsa_out = _dsa_attention(oiq, oiw, oik, osq, osk, osv, bsz, seq, k_top)
        x1, h2 = _out_proj(diff_out, dsa_out, w_o[l].astype(BF16), x2, mod3,
                           vec(g_attn_post), vec(g_ffn_pre), seq)
        x2 = _ffn(h2, w_gate[l].astype(BF16), w_up[l].astype(BF16), w_down[l].astype(BF16),
                  x1, mod3, vec(g_ffn_post), seq)
    return x2.reshape(bsz, seq, d)
```

```python
import functools
import math

import jax
import jax.numpy as jnp
from jax import lax
from jax.experimental import pallas as pl
from jax.experimental.pallas import tpu as pltpu

N_DIFF_HEADS = 8
DIFF_QK_DIM = 64
DIFF_V_DIM = 128
N_DSA_HEADS = 8
DSA_HEAD_DIM = 128
N_IDX_HEADS = 16
IDX_DIM = 64
INDEX_TOPK = 256
ROPE_THETA = 500000.0
ROPE_FRACTION = 4
ADA_CHUNKS = 6
RMS_EPS = 1e-6

DIFF_Q = N_DIFF_HEADS * 2 * DIFF_QK_DIM
DIFF_K = DIFF_Q
DIFF_V = N_DIFF_HEADS * DIFF_V_DIM
DSA_Q = N_DSA_HEADS * DSA_HEAD_DIM
DSA_K = DSA_HEAD_DIM
DSA_V = DSA_HEAD_DIM
IDX_Q = N_IDX_HEADS * IDX_DIM
IDX_K = IDX_DIM
IDX_W = N_IDX_HEADS
PROJ_SIZES = (DIFF_Q, DIFF_K, DIFF_V, DSA_Q, DSA_K, DSA_V, IDX_Q, IDX_K, IDX_W)

LANES = 128
INT16_TILE_ROWS = 16
MXU_WIDTH = 256
VMEM_LIMIT_BYTES = 56 * 1024 * 1024

ADA_TK, ADA_TN = 1024, 2048
ROPE_TM = 1024
ROW_TM = 512
OUT_SUB = 2
FFN_TF = 512
DIFF_TQ = 1024
DIFF_TK = 1024
DIFF_DIAG_TK = 256
DIFF_HPB = 2
DSA_TQS = 512
DSA_TQA = 256
DSA_TK = 512

F32 = jnp.float32
BF16 = jnp.bfloat16
INT_MIN = -(2 ** 31)
NEG_BIG = -1e30
LOG2E = math.log2(math.e)


def _cparams(sem):
    return pltpu.CompilerParams(dimension_semantics=sem, vmem_limit_bytes=VMEM_LIMIT_BYTES)


def _resident(shape, index_map):
    return pl.BlockSpec(shape, index_map, pipeline_mode=pl.Buffered(1))


def _rms(x, g):
    ms = jnp.mean(x * x, axis=-1, keepdims=True)
    return x * lax.rsqrt(ms + RMS_EPS) * g


def _ada_kernel(ct_ref, w_ref, b_ref, o_ref):
    k = pl.program_id(1)

    @pl.when(k == 0)
    def _():
        o_ref[...] = jnp.broadcast_to(b_ref[...], o_ref.shape)

    cv = ct_ref[...]
    s = cv * jax.nn.sigmoid(cv)
    w = w_ref[...]
    for b in range(o_ref.shape[0]):
        o_ref[b:b + 1, :] += jnp.sum(w * s[:, b:b + 1], axis=0, keepdims=True)


def _ada(c, w_ada, b_ada):
    bsz, d = c.shape
    n = w_ada.shape[1]
    tk, tn = ADA_TK, ADA_TN
    return pl.pallas_call(
        _ada_kernel,
        grid=(n // tn, d // tk),
        in_specs=[
            pl.BlockSpec((tk, bsz), lambda j, k: (k, 0)),
            pl.BlockSpec((tk, tn), lambda j, k: (k, j)),
            pl.BlockSpec((1, tn), lambda j, k: (0, j)),
        ],
        out_specs=pl.BlockSpec((bsz, tn), lambda j, k: (0, j)),
        out_shape=jax.ShapeDtypeStruct((bsz, n), F32),
        compiler_params=_cparams(("parallel", "arbitrary")),
        name="ada",
    )(c.T, w_ada, b_ada.reshape(1, n))


def _rope_kernel(pos_ref, c64_ref, s64_ref, c128_ref, s128_ref):
    pos = pos_ref[...]
    lane = lax.broadcasted_iota(jnp.int32, (1, LANES), 1)
    r64, r128 = 64 // ROPE_FRACTION, 128 // ROPE_FRACTION
    in64 = lane < r64
    in128 = jnp.logical_and(lane >= r64, lane < r64 + r128)
    k = jnp.where(in64, lane % (r64 // 2), (lane - r64) % (r128 // 2)).astype(F32)
    rot = jnp.where(in64, float(r64), float(r128))
    inv = jnp.power(jnp.float32(ROPE_THETA), -(k * 2.0 / rot))
    inv = jnp.where(jnp.logical_or(in64, in128), inv, 0.0)
    ang = pos * inv
    cos, sin = jnp.cos(ang), jnp.sin(ang)
    second = jnp.logical_and(lane >= 64, lane < 64 + r64)
    sign64 = jnp.where(lane % 64 < r64 // 2, -1.0, 1.0)
    c64_ref[...] = jnp.where(in64, cos, jnp.where(second, pltpu.roll(cos, 64, axis=1), 1.0))
    s64_ref[...] = jnp.where(in64, sin, jnp.where(second, pltpu.roll(sin, 64, axis=1), 0.0)) * sign64
    first = lane < r128
    sign128 = jnp.where(lane < r128 // 2, -1.0, 1.0)
    c128_ref[...] = jnp.where(first, pltpu.roll(cos, LANES - r64, axis=1), 1.0)
    s128_ref[...] = jnp.where(first, pltpu.roll(sin, LANES - r64, axis=1), 0.0) * sign128


def _rope_tables(positions):
    bsz, s = positions.shape
    m = bsz * s
    posb = jnp.broadcast_to(positions.reshape(m, 1).astype(F32), (m, LANES))
    tm = min(ROPE_TM, m)
    spec = pl.BlockSpec((tm, LANES), lambda i: (i, 0))
    shp = jax.ShapeDtypeStruct((m, LANES), F32)
    return pl.pallas_call(
        _rope_kernel,
        grid=(m // tm,),
        in_specs=[spec],
        out_specs=[spec] * 4,
        out_shape=[shp] * 4,
        compiler_params=_cparams(("parallel",)),
        name="rope",
    )(posb)


def _rope_apply(y, c_tab, s_tab, hd):
    n = y.shape[1]
    reps = n // LANES
    half = hd // ROPE_FRACTION // 2
    lane = lax.broadcasted_iota(jnp.int32, (1, n), 1) % hd
    cc = jnp.tile(c_tab, (1, reps)) if reps > 1 else c_tab
    ss = jnp.tile(s_tab, (1, reps)) if reps > 1 else s_tab
    ahead = pltpu.roll(y, n - half, axis=1)
    behind = pltpu.roll(y, half, axis=1)
    partner = jnp.where(lane < half, ahead, behind)
    return y * cc + partner * ss


def _proj_kernel(x_ref, g_ref, mod_ref, c64_ref, s64_ref, c128_ref, s128_ref, wmain, wtail,
                 odq, odk, odv, osq, osk, osv, oiq, oik, oiw):
    x = x_ref[...]
    h = _rms(x, g_ref[...]) * (1.0 + mod_ref[0, 1:2, :]) + mod_ref[0, 0:1, :]
    hb = h.astype(BF16)
    c64, s64 = c64_ref[...], s64_ref[...]
    c128, s128 = c128_ref[...], s128_ref[...]
    rope64 = lambda y: _rope_apply(y, c64, s64, 64)
    rope128 = lambda y: _rope_apply(y, c128, s128, 128)
    diff_scale = DIFF_QK_DIM ** -0.5 * LOG2E
    dsa_scale = DSA_HEAD_DIM ** -0.5 * LOG2E
    idx_w_scale = (N_IDX_HEADS ** -0.5) * (IDX_DIM ** -0.5)

    def tiles(w_ref, start, n):
        for c0 in range(0, n, MXU_WIDTH):
            y = jnp.dot(hb, w_ref[:, start + c0:start + c0 + MXU_WIDTH], preferred_element_type=F32)
            yield c0, y

    offs = [0]
    for sz in PROJ_SIZES:
        offs.append(offs[-1] + sz)
    for c0, y in tiles(wmain, offs[0], DIFF_Q):
        odq[:, c0:c0 + MXU_WIDTH] = (rope64(y) * diff_scale).astype(odq.dtype)
    for c0, y in tiles(wmain, offs[1], DIFF_K):
        odk[:, c0:c0 + MXU_WIDTH] = rope64(y).astype(odk.dtype)
    for c0, y in tiles(wmain, offs[2], DIFF_V):
        odv[:, c0:c0 + MXU_WIDTH] = y.astype(odv.dtype)
    for c0, y in tiles(wmain, offs[3], DSA_Q):
        y = rope128(y) * dsa_scale
        for j in range(MXU_WIDTH // DSA_HEAD_DIM):
            osq[c0 // DSA_HEAD_DIM + j] = y[:, j * DSA_HEAD_DIM:(j + 1) * DSA_HEAD_DIM].astype(osq.dtype)
    for c0, y in tiles(wmain, offs[4], DSA_K + DSA_V):
        osk[...] = rope128(y[:, :DSA_K]).astype(osk.dtype)
        osv[...] = y[:, DSA_K:].astype(osv.dtype)
    for c0, y in tiles(wmain, offs[6], IDX_Q):
        oiq[:, c0:c0 + MXU_WIDTH] = rope64(y).astype(oiq.dtype)
    for c0, y in tiles(wtail, 0, 2 * LANES):
        oik[...] = rope64(y[:, :LANES]).astype(oik.dtype)
        oiw[...] = y[:, LANES:] * idx_w_scale


def _proj(x2, g, mod3, tabs, wparts, seq):
    m, d = x2.shape
    tm = min(ROW_TM, seq)
    tiles_per_batch = seq // tm
    row = lambda n: pl.BlockSpec((tm, n), lambda i: (i, 0))
    in_specs = [
        row(d),
        _resident((1, d), lambda i: (0, 0)),
        pl.BlockSpec((1, ADA_CHUNKS, d), lambda i: (i // tiles_per_batch, 0, 0)),
        row(LANES), row(LANES), row(LANES), row(LANES),
    ] + [_resident(w.shape, lambda i: (0, 0)) for w in wparts]
    out_specs = [
        row(DIFF_Q), row(DIFF_K), row(DIFF_V),
        pl.BlockSpec((N_DSA_HEADS, tm, DSA_HEAD_DIM), lambda i: (0, i, 0)),
        row(LANES), row(LANES), row(IDX_Q), row(LANES), row(LANES),
    ]
    sds = jax.ShapeDtypeStruct
    out_shape = [
        sds((m, DIFF_Q), BF16), sds((m, DIFF_K), BF16), sds((m, DIFF_V), BF16),
        sds((N_DSA_HEADS, m, DSA_HEAD_DIM), BF16),
        sds((m, LANES), BF16), sds((m, LANES), BF16), sds((m, IDX_Q), BF16),
        sds((m, LANES), BF16), sds((m, LANES), F32),
    ]
    return pl.pallas_call(
        _proj_kernel,
        grid=(m // tm,),
        in_specs=in_specs,
        out_specs=out_specs,
        out_shape=out_shape,
        compiler_params=_cparams(("parallel",)),
        name="proj",
    )(x2, g, mod3, *tabs, *wparts)


def _diff_kernel(q_ref, k_ref, v_ref, lq1, lk1, lq2, lk2, g_ref, o_ref,
                 m_s, l_s, a_s, *, tq, tk, hpb, lam_init):
    qi = pl.program_id(2)
    lane = lax.broadcasted_iota(jnp.int32, (1, LANES), 1)
    qs = []
    for h in range(hpb):
        q = q_ref[:, h * LANES:(h + 1) * LANES]
        zero = jnp.zeros_like(q)
        qs.append(jnp.where(lane < DIFF_QK_DIM, q, zero))
        qs.append(jnp.where(lane >= DIFF_QK_DIM, q, zero))
    m_s[...] = jnp.full(m_s.shape, NEG_BIG, F32)
    l_s[...] = jnp.zeros(l_s.shape, F32)
    a_s[...] = jnp.zeros(a_s.shape, F32)

    def chunk(kstart, kw, r0, diag_offset):
        nr = tq - r0
        for h in range(hpb):
            kk = k_ref[pl.ds(kstart, kw), h * LANES:(h + 1) * LANES]
            vv = v_ref[pl.ds(kstart, kw), h * LANES:(h + 1) * LANES]
            for mp in range(2):
                i = 2 * h + mp
                s = lax.dot_general(qs[i][r0:, :], kk, (((1,), (1,)), ((), ())),
                                    preferred_element_type=F32)
                if diag_offset is not None:
                    r = lax.broadcasted_iota(jnp.int32, (nr, kw), 0) + r0
                    c = lax.broadcasted_iota(jnp.int32, (nr, kw), 1) + diag_offset
                    s = jnp.where(c <= r, s, -jnp.inf)
                m_old = m_s[i, r0:, :]
                m_new = jnp.maximum(m_old, jnp.max(s, axis=1, keepdims=True))
                p = jnp.exp2(s - jnp.tile(m_new, (1, kw // LANES)))
                alpha = jnp.exp2(m_old - m_new)
                l_s[i, r0:, :] = alpha * l_s[i, r0:, :] + jnp.sum(p, axis=1, keepdims=True)
                a_s[i, r0:, :] = alpha * a_s[i, r0:, :] + jnp.dot(
                    p.astype(BF16), vv, preferred_element_type=F32)
                m_s[i, r0:, :] = m_new

    per = tq // tk
    def body(i, carry):
        chunk(pl.multiple_of(i * tk, tk), tk, 0, None)
        return carry
    lax.fori_loop(0, qi * per, body, 0)
    dk = min(DIFF_DIAG_TK, tk)
    for j in range(tq // dk):
        chunk(pl.multiple_of(qi * tq + j * dk, dk), dk, j * dk, j * dk)

    lam = (jnp.exp(jnp.sum(lq1[...] * lk1[...], axis=1, keepdims=True))
           - jnp.exp(jnp.sum(lq2[...] * lk2[...], axis=1, keepdims=True)) + lam_init)
    for h in range(hpb):
        out = a_s[2 * h] / l_s[2 * h] - lam * (a_s[2 * h + 1] / l_s[2 * h + 1])
        o_ref[:, h * LANES:(h + 1) * LANES] = (
            _rms(out, g_ref[...]) * (1.0 - lam_init)).astype(o_ref.dtype)


def _diff_attention(dq, dk, dv, lams, g_sub, bsz, seq, lam_init):
    m = bsz * seq
    tq = min(DIFF_TQ, seq)
    tk = min(DIFF_TK, tq)
    hpb = DIFF_HPB
    nq = seq // tq
    w = hpb * LANES
    qspec = pl.BlockSpec((tq, w), lambda b, h, i: (b * nq + i, h))
    kvspec = pl.BlockSpec((seq, w), lambda b, h, i: (b, h))
    vec = lambda n: pl.BlockSpec((1, n), lambda b, h, i: (0, 0))
    kern = functools.partial(_diff_kernel, tq=tq, tk=tk, hpb=hpb, lam_init=lam_init)
    return pl.pallas_call(
        kern,
        grid=(bsz, N_DIFF_HEADS // hpb, nq),
        in_specs=[qspec, kvspec, kvspec] + [vec(DIFF_QK_DIM)] * 4 + [vec(DIFF_V_DIM)],
        out_specs=qspec,
        out_shape=jax.ShapeDtypeStruct((m, DIFF_V), BF16),
        scratch_shapes=[pltpu.VMEM((2 * hpb, tq, LANES), F32)] * 3,
        compiler_params=_cparams(("parallel", "parallel", "arbitrary")),
        name="diff",
    )(dq, dk, dv, *lams, g_sub)


def _dsa_kernel(iq_ref, iw_ref, ik_ref, q_ref, k_ref, v_ref, o_ref,
                keys, hi_t, lo_t, sel_t, halves, wrep, thr_s, m_s, l_s, acc,
                *, tqs, tqa, tk, k_top, seq):
    qi = pl.program_id(1)
    n_chunks = (qi * tqs + tqs + tk - 1) // tk
    lt = tk // LANES
    pk = INT16_TILE_ROWS
    row = qi * tqs + lax.broadcasted_iota(jnp.int32, (tqs, tk), 0)
    col0 = lax.broadcasted_iota(jnp.int32, (tqs, tk), 1)
    lane = lax.broadcasted_iota(jnp.int32, (1, LANES), 1)

    zero = jnp.zeros((tqs, LANES), BF16)
    for j in range(N_IDX_HEADS // 2):
        pair = iq_ref[:, j * LANES:(j + 1) * LANES]
        halves[2 * j] = jnp.where(lane < IDX_DIM, pair, zero)
        halves[2 * j + 1] = jnp.where(lane >= IDX_DIM, pair, zero)
    iw = iw_ref[...]
    for h in range(N_IDX_HEADS):
        wrep[h] = jnp.broadcast_to(iw[:, h:h + 1], (tqs, LANES))

    def score_chunk(c, carry):
        kstart = pl.multiple_of(c * tk, tk)
        ik = ik_ref[pl.ds(kstart, tk), :]
        sc = jnp.zeros((tqs, tk), F32)
        for h in range(N_IDX_HEADS):
            lg = lax.dot_general(halves[h], ik, (((1,), (1,)), ((), ())),
                                 preferred_element_type=F32)
            sc = sc + jnp.tile(wrep[h], (1, lt)) * jnp.maximum(lg, 0.0)
        bits = lax.bitcast_convert_type(sc, jnp.int32)
        key = bits ^ ((bits >> 31) & jnp.int32(0x7FFFFFFF))
        key = jnp.where(col0 + kstart <= row, key, jnp.int32(INT_MIN))
        keys[:, pl.ds(kstart, tk)] = key
        key_t = key.T
        hi_t[pl.ds(kstart, tk), :] = (key_t >> 16).astype(jnp.int16)
        lo_t[pl.ds(kstart, tk), :] = ((key_t & 0xFFFF) - 32768).astype(jnp.int16)
        return carry
    lax.fori_loop(0, n_chunks, score_chunk, 0)

    def count16(src, thr, strict=False):
        t16 = jnp.broadcast_to(thr, (pk, tqs)).astype(jnp.int16)
        def body(c, cnt):
            kstart = pl.multiple_of(c * tk, tk)
            blk = src[pl.ds(kstart, tk), :].reshape(tk // pk, pk, tqs)
            ok = (blk > t16[None]) if strict else (blk >= t16[None])
            hit = jnp.where(ok, jnp.int16(1), jnp.int16(0))
            part = hit[0]
            for g in range(1, tk // pk):
                part = part + hit[g]
            return cnt + part
        cnt = lax.fori_loop(0, n_chunks, body, jnp.zeros((pk, tqs), jnp.int16))
        return jnp.sum(cnt.astype(F32), axis=0, keepdims=True).astype(jnp.int32)

    def bisect16(src, need):
        def step(it, tau):
            cand = tau + jnp.left_shift(jnp.int32(1), 15 - it)
            return jnp.where(count16(src, cand) >= need, cand, tau)
        return lax.fori_loop(0, 16, step, jnp.full((1, tqs), -32768, jnp.int32))

    tau_hi = bisect16(hi_t, k_top)
    n_above = count16(hi_t, tau_hi, strict=True)
    t16 = jnp.broadcast_to(tau_hi, (pk, tqs)).astype(jnp.int16)
    def low_chunk(c, carry):
        kstart = pl.multiple_of(c * tk, tk)
        same = hi_t[pl.ds(kstart, tk), :].reshape(tk // pk, pk, tqs) == t16[None]
        lo = lo_t[pl.ds(kstart, tk), :].reshape(tk // pk, pk, tqs)
        sel_t[pl.ds(kstart, tk), :] = jnp.where(same, lo, jnp.int16(-32768)).reshape(tk, tqs)
        return carry
    lax.fori_loop(0, n_chunks, low_chunk, 0)
    tau_lo = bisect16(sel_t, k_top - n_above)
    n_ge = n_above + count16(sel_t, tau_lo)
    tau_row = tau_hi * 65536 + (tau_lo + 32768)
    tied_row = jnp.logical_and(n_ge > k_top, tau_row > INT_MIN)
    any_tied = jnp.max(jnp.where(tied_row, 1, 0)) > 0
    thr_s[...] = jnp.broadcast_to(tau_row, (LANES, tqs)).T

    @pl.when(any_tied)
    def _():
        tau = thr_s[:, 0:1]
        def count_where(pred):
            def body(c, total):
                kstart = pl.multiple_of(c * tk, tk)
                hit = jnp.where(pred(keys[:, pl.ds(kstart, tk)], col0 + kstart), 1, 0)
                return total + jnp.sum(hit, axis=1, keepdims=True)
            return lax.fori_loop(0, n_chunks, body, jnp.zeros((tqs, 1), jnp.int32))

        n_gt = count_where(lambda kc, col: kc > tau)
        n_all = count_where(lambda kc, col: kc >= tau)
        tied = jnp.logical_and(n_all > k_top, tau > INT_MIN)
        need = k_top - n_gt
        nbits = max(1, (seq - 1).bit_length())
        def idx_step(it, j):
            cand = j + jnp.left_shift(jnp.int32(1), nbits - 1 - it)
            below = count_where(lambda kc, col: jnp.logical_and(kc == tau, col < cand))
            return jnp.where(below < need, cand, j)
        jmax = lax.fori_loop(0, nbits, idx_step, jnp.zeros((tqs, 1), jnp.int32))

        def demote(c, carry):
            kstart = pl.multiple_of(c * tk, tk)
            kc = keys[:, pl.ds(kstart, tk)]
            drop = jnp.logical_and(jnp.logical_and(kc == tau, col0 + kstart > jmax), tied)
            keys[:, pl.ds(kstart, tk)] = jnp.where(drop, jnp.int32(INT_MIN), kc)
            return carry
        lax.fori_loop(0, n_chunks, demote, 0)

    thr_s[...] = jnp.maximum(thr_s[...], jnp.int32(INT_MIN + 1))

    nh = N_DSA_HEADS
    def attn_block(j, carry):
        r0 = pl.multiple_of(j * tqa, tqa)
        q_all = q_ref[:, pl.ds(r0, tqa), :].reshape(nh * tqa, DSA_HEAD_DIM)
        thr = jnp.tile(thr_s[pl.ds(r0, tqa), :], (1, lt))
        m_s[...] = jnp.full(m_s.shape, NEG_BIG, F32)
        l_s[...] = jnp.zeros(l_s.shape, F32)
        acc[...] = jnp.zeros(acc.shape, F32)

        def attn_chunk(c, carry2):
            kstart = pl.multiple_of(c * tk, tk)
            kk = k_ref[pl.ds(kstart, tk), :]
            vv = v_ref[pl.ds(kstart, tk), :]
            sel = keys[pl.ds(r0, tqa), pl.ds(kstart, tk)] >= thr
            bias = jnp.where(sel, 0.0, -jnp.inf)
            s = lax.dot_general(q_all, kk, (((1,), (1,)), ((), ())), preferred_element_type=F32)
            s = (s.reshape(nh, tqa, tk) + bias[None]).reshape(nh * tqa, tk)
            m_old = m_s[...]
            m_new = jnp.maximum(m_old, jnp.max(s, axis=1, keepdims=True))
            p = jnp.exp2(s - jnp.tile(m_new, (1, lt)))
            alpha = jnp.exp2(m_old - m_new)
            l_s[...] = alpha * l_s[...] + jnp.sum(p, axis=1, keepdims=True)
            acc[...] = alpha * acc[...] + jnp.dot(p.astype(BF16), vv, preferred_element_type=F32)
            m_s[...] = m_new
            return carry2
        lax.fori_loop(0, n_chunks, attn_chunk, 0)

        out = (acc[...] / l_s[...]).reshape(nh, tqa, DSA_HEAD_DIM)
        for h in range(nh):
            o_ref[pl.ds(r0, tqa), h * DSA_HEAD_DIM:(h + 1) * DSA_HEAD_DIM] = out[h].astype(o_ref.dtype)
        return carry
    lax.fori_loop(0, tqs // tqa, attn_block, 0)


def _dsa_attention(iq, iw, ik2, sq, sk, sv, bsz, seq, k_top):
    m = bsz * seq
    tqs = min(DSA_TQS, seq)
    tqa = min(DSA_TQA, tqs)
    tk = min(DSA_TK, seq)
    nq = seq // tqs
    rowspec = lambda n: pl.BlockSpec((tqs, n), lambda b, i: (b * nq + i, 0))
    seqspec = pl.BlockSpec((seq, LANES), lambda b, i: (b, 0))
    kern = functools.partial(_dsa_kernel, tqs=tqs, tqa=tqa, tk=tk, k_top=k_top, seq=seq)
    return pl.pallas_call(
        kern,
        grid=(bsz, nq),
        in_specs=[
            rowspec(IDX_Q), rowspec(LANES), seqspec,
            pl.BlockSpec((N_DSA_HEADS, tqs, DSA_HEAD_DIM), lambda b, i: (0, b * nq + i, 0)),
            seqspec, seqspec,
        ],
        out_specs=rowspec(DSA_Q),
        out_shape=jax.ShapeDtypeStruct((m, DSA_Q), BF16),
        scratch_shapes=[
            pltpu.VMEM((tqs, seq), jnp.int32),
            pltpu.VMEM((seq, tqs), jnp.int16),
            pltpu.VMEM((seq, tqs), jnp.int16),
            pltpu.VMEM((seq, tqs), jnp.int16),
            pltpu.VMEM((N_IDX_HEADS, tqs, LANES), BF16),
            pltpu.VMEM((N_IDX_HEADS, tqs, LANES), F32),
            pltpu.VMEM((tqs, LANES), jnp.int32),
            pltpu.VMEM((N_DSA_HEADS * tqa, LANES), F32),
            pltpu.VMEM((N_DSA_HEADS * tqa, LANES), F32),
            pltpu.VMEM((N_DSA_HEADS * tqa, DSA_HEAD_DIM), F32),
        ],
        compiler_params=_cparams(("parallel", "arbitrary")),
        name="dsa",
    )(iq, iw, ik2, sq, sk, sv)


def _out_kernel(a_ref, b_ref, wo_ref, x_ref, mod_ref, gpost_ref, gpre_ref, x1_ref, h2_ref):
    ka = a_ref.shape[1]
    ts = a_ref.shape[0] // OUT_SUB
    for r in range(OUT_SUB):
        rows = slice(r * ts, (r + 1) * ts)
        mixed = (jnp.dot(a_ref[rows, :], wo_ref[0:ka, :], preferred_element_type=F32)
                 + jnp.dot(b_ref[rows, :], wo_ref[ka:, :], preferred_element_type=F32))
        x1 = x_ref[rows, :] + mod_ref[0, 2:3, :] * _rms(mixed, gpost_ref[...])
        x1_ref[rows, :] = x1
        h2 = _rms(x1, gpre_ref[...]) * (1.0 + mod_ref[0, 4:5, :]) + mod_ref[0, 3:4, :]
        h2_ref[rows, :] = h2.astype(h2_ref.dtype)


def _out_proj(diff_out, dsa_out, wo, x2, mod3, g_post, g_pre, seq):
    m, d = x2.shape
    tm = min(ROW_TM, seq)
    tiles_per_batch = seq // tm
    row = lambda n: pl.BlockSpec((tm, n), lambda i: (i, 0))
    return pl.pallas_call(
        _out_kernel,
        grid=(m // tm,),
        in_specs=[
            row(diff_out.shape[1]), row(dsa_out.shape[1]),
            _resident(wo.shape, lambda i: (0, 0)),
            row(d),
            pl.BlockSpec((1, ADA_CHUNKS, d), lambda i: (i // tiles_per_batch, 0, 0)),
            _resident((1, d), lambda i: (0, 0)),
            _resident((1, d), lambda i: (0, 0)),
        ],
        out_specs=[row(d), row(d)],
        out_shape=[jax.ShapeDtypeStruct((m, d), F32), jax.ShapeDtypeStruct((m, d), BF16)],
        compiler_params=_cparams(("parallel",)),
        name="out",
    )(diff_out, dsa_out, wo, x2, mod3, g_post, g_pre)


def _ffn_kernel(h_ref, wg_ref, wu_ref, wd_ref, x1_ref, mod_ref, g_ref, o_ref, acc):
    f = pl.program_id(1)

    @pl.when(f == 0)
    def _():
        acc[...] = jnp.zeros(acc.shape, F32)

    h = h_ref[...]
    gate = jnp.dot(h, wg_ref[...].astype(BF16), preferred_element_type=F32)
    up = jnp.dot(h, wu_ref[...].astype(BF16), preferred_element_type=F32)
    act = (gate * jax.nn.sigmoid(gate) * up).astype(BF16)
    acc[...] += jnp.dot(act, wd_ref[...].astype(BF16), preferred_element_type=F32)

    @pl.when(f == pl.num_programs(1) - 1)
    def _():
        o_ref[...] = x1_ref[...] + mod_ref[0, 5:6, :] * _rms(acc[...], g_ref[...])


def _ffn(h2, wg, wu, wd, x1, mod3, g_post, seq):
    m, d = x1.shape
    dff = wg.shape[1]
    tm = min(ROW_TM, seq)
    tf = FFN_TF
    tiles_per_batch = seq // tm
    row = pl.BlockSpec((tm, d), lambda i, f: (i, 0))
    return pl.pallas_call(
        _ffn_kernel,
        grid=(m // tm, dff // tf),
        in_specs=[
            row,
            pl.BlockSpec((d, tf), lambda i, f: (0, f)),
            pl.BlockSpec((d, tf), lambda i, f: (0, f)),
            pl.BlockSpec((tf, d), lambda i, f: (f, 0)),
            row,
            pl.BlockSpec((1, ADA_CHUNKS, d), lambda i, f: (i // tiles_per_batch, 0, 0)),
            _resident((1, d), lambda i, f: (0, 0)),
        ],
        out_specs=row,
        out_shape=jax.ShapeDtypeStruct((m, d), F32),
        scratch_shapes=[pltpu.VMEM((tm, d), F32)],
        compiler_params=_cparams(("parallel", "arbitrary")),
        name="ffn",
    )(h2, wg, wu, wd, x1, mod3, g_post)


def _split_w_in(w_in):
    offs = [0]
    for sz in PROJ_SIZES:
        offs.append(offs[-1] + sz)
    d = w_in.shape[0]
    w_main = w_in.astype(BF16)
    w_ik = w_in[:, offs[7]:offs[8]]
    w_iw = w_in[:, offs[8]:offs[9]]
    w_tail = jnp.concatenate(
        [w_ik, w_ik, w_iw, jnp.zeros((d, LANES - IDX_W), w_in.dtype)], axis=1).astype(BF16)
    return [w_main, w_tail]


def kernel(x, c, positions, w_ada, b_ada, g_attn_pre, g_attn_post, g_ffn_pre, g_ffn_post, w_in, lambda_q1, lambda_k1, lambda_q2, lambda_k2, g_diff_sub, w_o, w_gate, w_up, w_down):
    bsz, seq, d = x.shape
    depth = w_ada.shape[0]
    k_top = min(INDEX_TOPK, seq // 4)
    m = bsz * seq
    tabs = _rope_tables(positions)
    x2 = x.reshape(m, d)
    for l in range(depth):
        lam_init = 0.8 - 0.6 * math.exp(-0.3 * l)
        mod3 = _ada(c, w_ada[l], b_ada[l]).reshape(bsz, ADA_CHUNKS, d)
        vec = lambda a: a[l].reshape(1, -1)
        odq, odk, odv, osq, osk, osv, oiq, oik, oiw = _proj(
            x2, vec(g_attn_pre), mod3, tabs, _split_w_in(w_in[l]), seq)
        lams = [vec(lambda_q1), vec(lambda_k1), vec(lambda_q2), vec(lambda_k2)]
        diff_out = _diff_attention(odq, odk, odv, lams, vec(g_diff_sub), bsz, seq, lam_init)
        dsa_out = _dsa_attention(oiq, oiw, oik, osq, osk, osv, bsz, seq, k_top)
        x1, h2 = _out_proj(diff_out, dsa_out, w_o[l].astype(BF16), x2, mod3,
                           vec(g_attn_post), vec(g_ffn_pre), seq)
        x2 = _ffn(h2, w_gate[l].astype(BF16), w_up[l], w_down[l],
                  x1, mod3, vec(g_ffn_post), seq)
    return x2.reshape(bsz, seq, d)
```

```python
import functools
import math

import jax
import jax.numpy as jnp
from jax import lax
from jax.experimental import pallas as pl
from jax.experimental.pallas import tpu as pltpu

N_DIFF_HEADS = 8
DIFF_QK_DIM = 64
DIFF_V_DIM = 128
N_DSA_HEADS = 8
DSA_HEAD_DIM = 128
N_IDX_HEADS = 16
IDX_DIM = 64
INDEX_TOPK = 256
ROPE_THETA = 500000.0
ROPE_FRACTION = 4
ADA_CHUNKS = 6
RMS_EPS = 1e-6

DIFF_Q = N_DIFF_HEADS * 2 * DIFF_QK_DIM
DIFF_K = DIFF_Q
DIFF_V = N_DIFF_HEADS * DIFF_V_DIM
DSA_Q = N_DSA_HEADS * DSA_HEAD_DIM
DSA_K = DSA_HEAD_DIM
DSA_V = DSA_HEAD_DIM
IDX_Q = N_IDX_HEADS * IDX_DIM
IDX_K = IDX_DIM
IDX_W = N_IDX_HEADS
PROJ_SIZES = (DIFF_Q, DIFF_K, DIFF_V, DSA_Q, DSA_K, DSA_V, IDX_Q, IDX_K, IDX_W)

LANES = 128
INT16_TILE_ROWS = 16
MXU_WIDTH = 256
VMEM_LIMIT_BYTES = 56 * 1024 * 1024

ADA_TK, ADA_TN = 1024, 2048
ROPE_TM = 1024
ROW_TM = 512
OUT_SUB = 2
FFN_TF = 512
DIFF_TQ = 1024
DIFF_TK = 1024
DIFF_DIAG_TK = 256
DIFF_HPB = 2
DSA_TQS = 512
DSA_TQA = 256
DSA_TK = 512

F32 = jnp.float32
BF16 = jnp.bfloat16
INT_MIN = -(2 ** 31)
NEG_BIG = -1e30
LOG2E = math.log2(math.e)


def _cparams(sem):
    return pltpu.CompilerParams(dimension_semantics=sem, vmem_limit_bytes=VMEM_LIMIT_BYTES)


def _resident(shape, index_map):
    return pl.BlockSpec(shape, index_map, pipeline_mode=pl.Buffered(1))


def _rms(x, g):
    ms = jnp.mean(x * x, axis=-1, keepdims=True)
    return x * lax.rsqrt(ms + RMS_EPS) * g


def _ada_kernel(ct_ref, w_ref, b_ref, o_ref):
    k = pl.program_id(1)

    @pl.when(k == 0)
    def _():
        o_ref[...] = jnp.broadcast_to(b_ref[...], o_ref.shape)

    cv = ct_ref[...]
    s = cv * jax.nn.sigmoid(cv)
    w = w_ref[...]
    for b in range(o_ref.shape[0]):
        o_ref[b:b + 1, :] += jnp.sum(w * s[:, b:b + 1], axis=0, keepdims=True)


def _ada(c, w_ada, b_ada):
    bsz, d = c.shape
    n = w_ada.shape[1]
    tk, tn = ADA_TK, ADA_TN
    return pl.pallas_call(
        _ada_kernel,
        grid=(n // tn, d // tk),
        in_specs=[
            pl.BlockSpec((tk, bsz), lambda j, k: (k, 0)),
            pl.BlockSpec((tk, tn), lambda j, k: (k, j)),
            pl.BlockSpec((1, tn), lambda j, k: (0, j)),
        ],
        out_specs=pl.BlockSpec((bsz, tn), lambda j, k: (0, j)),
        out_shape=jax.ShapeDtypeStruct((bsz, n), F32),
        compiler_params=_cparams(("parallel", "arbitrary")),
        name="ada",
    )(c.T, w_ada, b_ada.reshape(1, n))


def _rope_kernel(pos_ref, c64_ref, s64_ref, c128_ref, s128_ref):
    pos = pos_ref[...]
    lane = lax.broadcasted_iota(jnp.int32, (1, LANES), 1)
    r64, r128 = 64 // ROPE_FRACTION, 128 // ROPE_FRACTION
    in64 = lane < r64
    in128 = jnp.logical_and(lane >= r64, lane < r64 + r128)
    k = jnp.where(in64, lane % (r64 // 2), (lane - r64) % (r128 // 2)).astype(F32)
    rot = jnp.where(in64, float(r64), float(r128))
    inv = jnp.power(jnp.float32(ROPE_THETA), -(k * 2.0 / rot))
    inv = jnp.where(jnp.logical_or(in64, in128), inv, 0.0)
    ang = pos * inv
    cos, sin = jnp.cos(ang), jnp.sin(ang)
    second = jnp.logical_and(lane >= 64, lane < 64 + r64)
    sign64 = jnp.where(lane % 64 < r64 // 2, -1.0, 1.0)
    c64_ref[...] = jnp.where(in64, cos, jnp.where(second, pltpu.roll(cos, 64, axis=1), 1.0))
    s64_ref[...] = jnp.where(in64, sin, jnp.where(second, pltpu.roll(sin, 64, axis=1), 0.0)) * sign64
    first = lane < r128
    sign128 = jnp.where(lane < r128 // 2, -1.0, 1.0)
    c128_ref[...] = jnp.where(first, pltpu.roll(cos, LANES - r64, axis=1), 1.0)
    s128_ref[...] = jnp.where(first, pltpu.roll(sin, LANES - r64, axis=1), 0.0) * sign128


def _rope_tables(positions):
    bsz, s = positions.shape
    m = bsz * s
    posb = jnp.broadcast_to(positions.reshape(m, 1).astype(F32), (m, LANES))
    tm = min(ROPE_TM, m)
    spec = pl.BlockSpec((tm, LANES), lambda i: (i, 0))
    shp = jax.ShapeDtypeStruct((m, LANES), F32)
    return pl.pallas_call(
        _rope_kernel,
        grid=(m // tm,),
        in_specs=[spec],
        out_specs=[spec] * 4,
        out_shape=[shp] * 4,
        compiler_params=_cparams(("parallel",)),
        name="rope",
    )(posb)


def _rope_apply(y, c_tab, s_tab, hd):
    n = y.shape[1]
    reps = n // LANES
    half = hd // ROPE_FRACTION // 2
    lane = lax.broadcasted_iota(jnp.int32, (1, n), 1) % hd
    cc = jnp.tile(c_tab, (1, reps)) if reps > 1 else c_tab
    ss = jnp.tile(s_tab, (1, reps)) if reps > 1 else s_tab
    ahead = pltpu.roll(y, n - half, axis=1)
    behind = pltpu.roll(y, half, axis=1)
    partner = jnp.where(lane < half, ahead, behind)
    return y * cc + partner * ss


def _proj_kernel(x_ref, g_ref, mod_ref, c64_ref, s64_ref, c128_ref, s128_ref, wmain, wtail,
                 odq, odk, odv, osq, osk, osv, oiq, oik, oiw):
    x = x_ref[...]
    h = _rms(x, g_ref[...]) * (1.0 + mod_ref[0, 1:2, :]) + mod_ref[0, 0:1, :]
    hb = h.astype(BF16)
    c64, s64 = c64_ref[...], s64_ref[...]
    c128, s128 = c128_ref[...], s128_ref[...]
    rope64 = lambda y: _rope_apply(y, c64, s64, 64)
    rope128 = lambda y: _rope_apply(y, c128, s128, 128)
    diff_scale = DIFF_QK_DIM ** -0.5 * LOG2E
    dsa_scale = DSA_HEAD_DIM ** -0.5 * LOG2E
    idx_w_scale = (N_IDX_HEADS ** -0.5) * (IDX_DIM ** -0.5)

    def tiles(w_ref, start, n):
        for c0 in range(0, n, MXU_WIDTH):
            y = jnp.dot(hb, w_ref[:, start + c0:start + c0 + MXU_WIDTH], preferred_element_type=F32)
            yield c0, y

    offs = [0]
    for sz in PROJ_SIZES:
        offs.append(offs[-1] + sz)
    for c0, y in tiles(wmain, offs[0], DIFF_Q):
        odq[:, c0:c0 + MXU_WIDTH] = (rope64(y) * diff_scale).astype(odq.dtype)
    for c0, y in tiles(wmain, offs[1], DIFF_K):
        odk[:, c0:c0 + MXU_WIDTH] = rope64(y).astype(odk.dtype)
    for c0, y in tiles(wmain, offs[2], DIFF_V):
        odv[:, c0:c0 + MXU_WIDTH] = y.astype(odv.dtype)
    for c0, y in tiles(wmain, offs[3], DSA_Q):
        y = rope128(y) * dsa_scale
        for j in range(MXU_WIDTH // DSA_HEAD_DIM):
            osq[c0 // DSA_HEAD_DIM + j] = y[:, j * DSA_HEAD_DIM:(j + 1) * DSA_HEAD_DIM].astype(osq.dtype)
    for c0, y in tiles(wmain, offs[4], DSA_K + DSA_V):
        osk[...] = rope128(y[:, :DSA_K]).astype(osk.dtype)
        osv[...] = y[:, DSA_K:].astype(osv.dtype)
    for c0, y in tiles(wmain, offs[6], IDX_Q):
        oiq[:, c0:c0 + MXU_WIDTH] = rope64(y).astype(oiq.dtype)
    for c0, y in tiles(wtail, 0, 2 * LANES):
        oik[...] = rope64(y[:, :LANES]).astype(oik.dtype)
        oiw[...] = y[:, LANES:] * idx_w_scale


def _proj(x2, g, mod3, tabs, wparts, seq):
    m, d = x2.shape
    tm = min(ROW_TM, seq)
    tiles_per_batch = seq // tm
    row = lambda n: pl.BlockSpec((tm, n), lambda i: (i, 0))
    in_specs = [
        row(d),
        _resident((1, d), lambda i: (0, 0)),
        pl.BlockSpec((1, ADA_CHUNKS, d), lambda i: (i // tiles_per_batch, 0, 0)),
        row(LANES), row(LANES), row(LANES), row(LANES),
    ] + [_resident(w.shape, lambda i: (0, 0)) for w in wparts]
    out_specs = [
        row(DIFF_Q), row(DIFF_K), row(DIFF_V),
        pl.BlockSpec((N_DSA_HEADS, tm, DSA_HEAD_DIM), lambda i: (0, i, 0)),
        row(LANES), row(LANES), row(IDX_Q), row(LANES), row(LANES),
    ]
    sds = jax.ShapeDtypeStruct
    out_shape = [
        sds((m, DIFF_Q), BF16), sds((m, DIFF_K), BF16), sds((m, DIFF_V), BF16),
        sds((N_DSA_HEADS, m, DSA_HEAD_DIM), BF16),
        sds((m, LANES), BF16), sds((m, LANES), BF16), sds((m, IDX_Q), BF16),
        sds((m, LANES), BF16), sds((m, LANES), F32),
    ]
    return pl.pallas_call(
        _proj_kernel,
        grid=(m // tm,),
        in_specs=in_specs,
        out_specs=out_specs,
        out_shape=out_shape,
        compiler_params=_cparams(("parallel",)),
        name="proj",
    )(x2, g, mod3, *tabs, *wparts)


def _diff_kernel(q_ref, k_ref, v_ref, lq1, lk1, lq2, lk2, g_ref, o_ref,
                 m_s, l_s, a_s, *, tq, tk, hpb, lam_init):
    qi = pl.program_id(2)
    lane = lax.broadcasted_iota(jnp.int32, (1, LANES), 1)
    qs = []
    for h in range(hpb):
        q = q_ref[:, h * LANES:(h + 1) * LANES]
        zero = jnp.zeros_like(q)
        qs.append(jnp.where(lane < DIFF_QK_DIM, q, zero))
        qs.append(jnp.where(lane >= DIFF_QK_DIM, q, zero))
    m_s[...] = jnp.full(m_s.shape, NEG_BIG, F32)
    l_s[...] = jnp.zeros(l_s.shape, F32)
    a_s[...] = jnp.zeros(a_s.shape, F32)

    def chunk(kstart, kw, r0, diag_offset):
        nr = tq - r0
        for h in range(hpb):
            kk = k_ref[pl.ds(kstart, kw), h * LANES:(h + 1) * LANES]
            vv = v_ref[pl.ds(kstart, kw), h * LANES:(h + 1) * LANES]
            for mp in range(2):
                i = 2 * h + mp
                s = lax.dot_general(qs[i][r0:, :], kk, (((1,), (1,)), ((), ())),
                                    preferred_element_type=F32)
                if diag_offset is not None:
                    r = lax.broadcasted_iota(jnp.int32, (nr, kw), 0) + r0
                    c = lax.broadcasted_iota(jnp.int32, (nr, kw), 1) + diag_offset
                    s = jnp.where(c <= r, s, -jnp.inf)
                m_old = m_s[i, r0:, :]
                m_new = jnp.maximum(m_old, jnp.max(s, axis=1, keepdims=True))
                p = jnp.exp2(s - jnp.tile(m_new, (1, kw // LANES)))
                alpha = jnp.exp2(m_old - m_new)
                l_s[i, r0:, :] = alpha * l_s[i, r0:, :] + jnp.sum(p, axis=1, keepdims=True)
                a_s[i, r0:, :] = alpha * a_s[i, r0:, :] + jnp.dot(
                    p.astype(BF16), vv, preferred_element_type=F32)
                m_s[i, r0:, :] = m_new

    per = tq // tk
    def body(i, carry):
        chunk(pl.multiple_of(i * tk, tk), tk, 0, None)
        return carry
    lax.fori_loop(0, qi * per, body, 0)
    dk = min(DIFF_DIAG_TK, tk)
    for j in range(tq // dk):
        chunk(pl.multiple_of(qi * tq + j * dk, dk), dk, j * dk, j * dk)

    lam = (jnp.exp(jnp.sum(lq1[...] * lk1[...], axis=1, keepdims=True))
           - jnp.exp(jnp.sum(lq2[...] * lk2[...], axis=1, keepdims=True)) + lam_init)
    for h in range(hpb):
        out = a_s[2 * h] / l_s[2 * h] - lam * (a_s[2 * h + 1] / l_s[2 * h + 1])
        o_ref[:, h * LANES:(h + 1) * LANES] = (
            _rms(out, g_ref[...]) * (1.0 - lam_init)).astype(o_ref.dtype)


def _diff_attention(dq, dk, dv, lams, g_sub, bsz, seq, lam_init):
    m = bsz * seq
    tq = min(DIFF_TQ, seq)
    tk = min(DIFF_TK, tq)
    hpb = DIFF_HPB
    nq = seq // tq
    w = hpb * LANES
    qspec = pl.BlockSpec((tq, w), lambda b, h, i: (b * nq + i, h))
    kvspec = pl.BlockSpec((seq, w), lambda b, h, i: (b, h))
    vec = lambda n: pl.BlockSpec((1, n), lambda b, h, i: (0, 0))
    kern = functools.partial(_diff_kernel, tq=tq, tk=tk, hpb=hpb, lam_init=lam_init)
    return pl.pallas_call(
        kern,
        grid=(bsz, N_DIFF_HEADS // hpb, nq),
        in_specs=[qspec, kvspec, kvspec] + [vec(DIFF_QK_DIM)] * 4 + [vec(DIFF_V_DIM)],
        out_specs=qspec,
        out_shape=jax.ShapeDtypeStruct((m, DIFF_V), BF16),
        scratch_shapes=[pltpu.VMEM((2 * hpb, tq, LANES), F32)] * 3,
        compiler_params=_cparams(("parallel", "parallel", "arbitrary")),
        name="diff",
    )(dq, dk, dv, *lams, g_sub)


def _dsa_kernel(iq_ref, iw_ref, ik_ref, q_ref, k_ref, v_ref, o_ref,
                keys, hi_t, lo_t, sel_t, halves, wrep, thr_s, m_s, l_s, acc,
                *, tqs, tqa, tk, k_top, seq):
    qi = pl.program_id(1)
    n_chunks = (qi * tqs + tqs + tk - 1) // tk
    lt = tk // LANES
    pk = INT16_TILE_ROWS
    row = qi * tqs + lax.broadcasted_iota(jnp.int32, (tqs, tk), 0)
    col0 = lax.broadcasted_iota(jnp.int32, (tqs, tk), 1)
    lane = lax.broadcasted_iota(jnp.int32, (1, LANES), 1)

    zero = jnp.zeros((tqs, LANES), BF16)
    for j in range(N_IDX_HEADS // 2):
        pair = iq_ref[:, j * LANES:(j + 1) * LANES]
        halves[2 * j] = jnp.where(lane < IDX_DIM, pair, zero)
        halves[2 * j + 1] = jnp.where(lane >= IDX_DIM, pair, zero)
    iw = iw_ref[...]
    for h in range(N_IDX_HEADS):
        wrep[h] = jnp.broadcast_to(iw[:, h:h + 1], (tqs, LANES))

    def score_chunk(c, carry):
        kstart = pl.multiple_of(c * tk, tk)
        ik = ik_ref[pl.ds(kstart, tk), :]
        sc = jnp.zeros((tqs, tk), F32)
        for h in range(N_IDX_HEADS):
            lg = lax.dot_general(halves[h], ik, (((1,), (1,)), ((), ())),
                                 preferred_element_type=F32)
            sc = sc + jnp.tile(wrep[h], (1, lt)) * jnp.maximum(lg, 0.0)
        bits = lax.bitcast_convert_type(sc, jnp.int32)
        key = bits ^ ((bits >> 31) & jnp.int32(0x7FFFFFFF))
        key = jnp.where(col0 + kstart <= row, key, jnp.int32(INT_MIN))
        keys[:, pl.ds(kstart, tk)] = key
        key_t = key.T
        hi_t[pl.ds(kstart, tk), :] = (key_t >> 16).astype(jnp.int16)
        lo_t[pl.ds(kstart, tk), :] = ((key_t & 0xFFFF) - 32768).astype(jnp.int16)
        return carry
    lax.fori_loop(0, n_chunks, score_chunk, 0)

    def count16(src, thr, strict=False):
        t16 = jnp.broadcast_to(thr, (pk, tqs)).astype(jnp.int16)
        def body(c, cnt):
            kstart = pl.multiple_of(c * tk, tk)
            blk = src[pl.ds(kstart, tk), :].reshape(tk // pk, pk, tqs)
            ok = (blk > t16[None]) if strict else (blk >= t16[None])
            hit = jnp.where(ok, jnp.int16(1), jnp.int16(0))
            part = hit[0]
            for g in range(1, tk // pk):
                part = part + hit[g]
            return cnt + part
        cnt = lax.fori_loop(0, n_chunks, body, jnp.zeros((pk, tqs), jnp.int16))
        return jnp.sum(cnt.astype(F32), axis=0, keepdims=True).astype(jnp.int32)

    def bisect16(src, need):
        def step(it, tau):
            cand = tau + jnp.left_shift(jnp.int32(1), 15 - it)
            return jnp.where(count16(src, cand) >= need, cand, tau)
        return lax.fori_loop(0, 16, step, jnp.full((1, tqs), -32768, jnp.int32))

    tau_hi = bisect16(hi_t, k_top)
    n_above = count16(hi_t, tau_hi, strict=True)
    t16 = jnp.broadcast_to(tau_hi, (pk, tqs)).astype(jnp.int16)
    def low_chunk(c, carry):
        kstart = pl.multiple_of(c * tk, tk)
        same = hi_t[pl.ds(kstart, tk), :].reshape(tk // pk, pk, tqs) == t16[None]
        lo = lo_t[pl.ds(kstart, tk), :].reshape(tk // pk, pk, tqs)
        sel_t[pl.ds(kstart, tk), :] = jnp.where(same, lo, jnp.int16(-32768)).reshape(tk, tqs)
        return carry
    lax.fori_loop(0, n_chunks, low_chunk, 0)
    tau_lo = bisect16(sel_t, k_top - n_above)
    n_ge = n_above + count16(sel_t, tau_lo)
    tau_row = tau_hi * 65536 + (tau_lo + 32768)
    tied_row = jnp.logical_and(n_ge > k_top, tau_row > INT_MIN)
    any_tied = jnp.max(jnp.where(tied_row, 1, 0)) > 0
    thr_s[...] = jnp.broadcast_to(tau_row, (LANES, tqs)).T

    @pl.when(any_tied)
    def _():
        tau = thr_s[:, 0:1]
        def count_where(pred):
            def body(c, total):
                kstart = pl.multiple_of(c * tk, tk)
                hit = jnp.where(pred(keys[:, pl.ds(kstart, tk)], col0 + kstart), 1, 0)
                return total + jnp.sum(hit, axis=1, keepdims=True)
            return lax.fori_loop(0, n_chunks, body, jnp.zeros((tqs, 1), jnp.int32))

        n_gt = count_where(lambda kc, col: kc > tau)
        n_all = count_where(lambda kc, col: kc >= tau)
        tied = jnp.logical_and(n_all > k_top, tau > INT_MIN)
        need = k_top - n_gt
        nbits = max(1, (seq - 1).bit_length())
        def idx_step(it, j):
            cand = j + jnp.left_shift(jnp.int32(1), nbits - 1 - it)
            below = count_where(lambda kc, col: jnp.logical_and(kc == tau, col < cand))
            return jnp.where(below < need, cand, j)
        jmax = lax.fori_loop(0, nbits, idx_step, jnp.zeros((tqs, 1), jnp.int32))

        def demote(c, carry):
            kstart = pl.multiple_of(c * tk, tk)
            kc = keys[:, pl.ds(kstart, tk)]
            drop = jnp.logical_and(jnp.logical_and(kc == tau, col0 + kstart > jmax), tied)
            keys[:, pl.ds(kstart, tk)] = jnp.where(drop, jnp.int32(INT_MIN), kc)
            return carry
        lax.fori_loop(0, n_chunks, demote, 0)

    thr_s[...] = jnp.maximum(thr_s[...], jnp.int32(INT_MIN + 1))

    nh = N_DSA_HEADS
    def attn_block(j, carry):
        r0 = pl.multiple_of(j * tqa, tqa)
        q_all = q_ref[:, pl.ds(r0, tqa), :].reshape(nh * tqa, DSA_HEAD_DIM)
        thr = jnp.tile(thr_s[pl.ds(r0, tqa), :], (1, lt))
        m_s[...] = jnp.full(m_s.shape, NEG_BIG, F32)
        l_s[...] = jnp.zeros(l_s.shape, F32)
        acc[...] = jnp.zeros(acc.shape, F32)

        def attn_chunk(c, carry2):
            kstart = pl.multiple_of(c * tk, tk)
            kk = k_ref[pl.ds(kstart, tk), :]
            vv = v_ref[pl.ds(kstart, tk), :]
            sel = keys[pl.ds(r0, tqa), pl.ds(kstart, tk)] >= thr
            bias = jnp.where(sel, 0.0, -jnp.inf)
            s = lax.dot_general(q_all, kk, (((1,), (1,)), ((), ())), preferred_element_type=F32)
            s = (s.reshape(nh, tqa, tk) + bias[None]).reshape(nh * tqa, tk)
            m_old = m_s[...]
            m_new = jnp.maximum(m_old, jnp.max(s, axis=1, keepdims=True))
            p = jnp.exp2(s - jnp.tile(m_new, (1, lt)))
            alpha = jnp.exp2(m_old - m_new)
            l_s[...] = alpha * l_s[...] + jnp.sum(p, axis=1, keepdims=True)
            acc[...] = alpha * acc[...] + jnp.dot(p.astype(BF16), vv, preferred_element_type=F32)
            m_s[...] = m_new
            return carry2
        lax.fori_loop(0, n_chunks, attn_chunk, 0)

        out = (acc[...] / l_s[...]).reshape(nh, tqa, DSA_HEAD_DIM)
        for h in range(nh):
            o_ref[pl.ds(r0, tqa), h * DSA_HEAD_DIM:(h + 1) * DSA_HEAD_DIM] = out[h].astype(o_ref.dtype)
        return carry
    lax.fori_loop(0, tqs // tqa, attn_block, 0)


def _dsa_attention(iq, iw, ik2, sq, sk, sv, bsz, seq, k_top):
    m = bsz * seq
    tqs = min(DSA_TQS, seq)
    tqa = min(DSA_TQA, tqs)
    tk = min(DSA_TK, seq)
    nq = seq // tqs
    rowspec = lambda n: pl.BlockSpec((tqs, n), lambda b, i: (b * nq + i, 0))
    seqspec = pl.BlockSpec((seq, LANES), lambda b, i: (b, 0))
    kern = functools.partial(_dsa_kernel, tqs=tqs, tqa=tqa, tk=tk, k_top=k_top, seq=seq)
    return pl.pallas_call(
        kern,
        grid=(bsz, nq),
        in_specs=[
            rowspec(IDX_Q), rowspec(LANES), seqspec,
            pl.BlockSpec((N_DSA_HEADS, tqs, DSA_HEAD_DIM), lambda b, i: (0, b * nq + i, 0)),
            seqspec, seqspec,
        ],
        out_specs=rowspec(DSA_Q),
        out_shape=jax.ShapeDtypeStruct((m, DSA_Q), BF16),
        scratch_shapes=[
            pltpu.VMEM((tqs, seq), jnp.int32),
            pltpu.VMEM((seq, tqs), jnp.int16),
            pltpu.VMEM((seq, tqs), jnp.int16),
            pltpu.VMEM((seq, tqs), jnp.int16),
            pltpu.VMEM((N_IDX_HEADS, tqs, LANES), BF16),
            pltpu.VMEM((N_IDX_HEADS, tqs, LANES), F32),
            pltpu.VMEM((tqs, LANES), jnp.int32),
            pltpu.VMEM((N_DSA_HEADS * tqa, LANES), F32),
            pltpu.VMEM((N_DSA_HEADS * tqa, LANES), F32),
            pltpu.VMEM((N_DSA_HEADS * tqa, DSA_HEAD_DIM), F32),
        ],
        compiler_params=_cparams(("parallel", "arbitrary")),
        name="dsa",
    )(iq, iw, ik2, sq, sk, sv)


def _out_kernel(a_ref, b_ref, wo_ref, x_ref, mod_ref, gpost_ref, gpre_ref, x1_ref, h2_ref):
    ka = a_ref.shape[1]
    ts = a_ref.shape[0] // OUT_SUB
    for r in range(OUT_SUB):
        rows = slice(r * ts, (r + 1) * ts)
        mixed = (jnp.dot(a_ref[rows, :], wo_ref[0:ka, :], preferred_element_type=F32)
                 + jnp.dot(b_ref[rows, :], wo_ref[ka:, :], preferred_element_type=F32))
        x1 = x_ref[rows, :] + mod_ref[0, 2:3, :] * _rms(mixed, gpost_ref[...])
        x1_ref[rows, :] = x1
        h2 = _rms(x1, gpre_ref[...]) * (1.0 + mod_ref[0, 4:5, :]) + mod_ref[0, 3:4, :]
        h2_ref[rows, :] = h2.astype(h2_ref.dtype)


def _out_proj(diff_out, dsa_out, wo, x2, mod3, g_post, g_pre, seq):
    m, d = x2.shape
    tm = min(ROW_TM, seq)
    tiles_per_batch = seq // tm
    row = lambda n: pl.BlockSpec((tm, n), lambda i: (i, 0))
    return pl.pallas_call(
        _out_kernel,
        grid=(m // tm,),
        in_specs=[
            row(diff_out.shape[1]), row(dsa_out.shape[1]),
            _resident(wo.shape, lambda i: (0, 0)),
            row(d),
            pl.BlockSpec((1, ADA_CHUNKS, d), lambda i: (i // tiles_per_batch, 0, 0)),
            _resident((1, d), lambda i: (0, 0)),
            _resident((1, d), lambda i: (0, 0)),
        ],
        out_specs=[row(d), row(d)],
        out_shape=[jax.ShapeDtypeStruct((m, d), F32), jax.ShapeDtypeStruct((m, d), BF16)],
        compiler_params=_cparams(("parallel",)),
        name="out",
    )(diff_out, dsa_out, wo, x2, mod3, g_post, g_pre)


def _ffn_kernel(h_ref, wg_ref, wu_ref, wd_ref, x1_ref, mod_ref, g_ref, o_ref, acc):
    f = pl.program_id(1)

    @pl.when(f == 0)
    def _():
        acc[...] = jnp.zeros(acc.shape, F32)

    h = h_ref[...]
    gate = jnp.dot(h, wg_ref[0], preferred_element_type=F32)
    up = jnp.dot(h, wu_ref[0], preferred_element_type=F32)
    act = (gate * jax.nn.sigmoid(gate) * up).astype(BF16)
    acc[...] += jnp.dot(act, wd_ref[...], preferred_element_type=F32)

    @pl.when(f == pl.num_programs(1) - 1)
    def _():
        o_ref[...] = x1_ref[...] + mod_ref[0, 5:6, :] * _rms(acc[...], g_ref[...])


def _ffn(h2, wg, wu, wd, x1, mod3, g_post, seq):
    m, d = x1.shape
    nf, _, tf = wg.shape
    tm = min(ROW_TM, seq)
    tiles_per_batch = seq // tm
    row = pl.BlockSpec((tm, d), lambda i, f: (i, 0))
    return pl.pallas_call(
        _ffn_kernel,
        grid=(m // tm, nf),
        in_specs=[
            row,
            pl.BlockSpec((1, d, tf), lambda i, f: (f, 0, 0)),
            pl.BlockSpec((1, d, tf), lambda i, f: (f, 0, 0)),
            pl.BlockSpec((tf, d), lambda i, f: (f, 0)),
            row,
            pl.BlockSpec((1, ADA_CHUNKS, d), lambda i, f: (i // tiles_per_batch, 0, 0)),
            _resident((1, d), lambda i, f: (0, 0)),
        ],
        out_specs=row,
        out_shape=jax.ShapeDtypeStruct((m, d), F32),
        scratch_shapes=[pltpu.VMEM((tm, d), F32)],
        compiler_params=_cparams(("parallel", "arbitrary")),
        name="ffn",
    )(h2, wg, wu, wd, x1, mod3, g_post)


def _split_w_in(w_in):
    offs = [0]
    for sz in PROJ_SIZES:
        offs.append(offs[-1] + sz)
    d = w_in.shape[0]
    w_main = w_in.astype(BF16)
    w_ik = w_in[:, offs[7]:offs[8]]
    w_iw = w_in[:, offs[8]:offs[9]]
    w_tail = jnp.concatenate(
        [w_ik, w_ik, w_iw, jnp.zeros((d, LANES - IDX_W), w_in.dtype)], axis=1).astype(BF16)
    return [w_main, w_tail]


def _col_tiles(w):
    d, n = w.shape
    return w.astype(BF16).reshape(d, n // FFN_TF, FFN_TF).transpose(1, 0, 2)


def kernel(x, c, positions, w_ada, b_ada, g_attn_pre, g_attn_post, g_ffn_pre, g_ffn_post, w_in, lambda_q1, lambda_k1, lambda_q2, lambda_k2, g_diff_sub, w_o, w_gate, w_up, w_down):
    bsz, seq, d = x.shape
    depth = w_ada.shape[0]
    k_top = min(INDEX_TOPK, seq // 4)
    m = bsz * seq
    tabs = _rope_tables(positions)
    x2 = x.reshape(m, d)
    for l in range(depth):
        lam_init = 0.8 - 0.6 * math.exp(-0.3 * l)
        mod3 = _ada(c, w_ada[l], b_ada[l]).reshape(bsz, ADA_CHUNKS, d)
        vec = lambda a: a[l].reshape(1, -1)
        odq, odk, odv, osq, osk, osv, oiq, oik, oiw = _proj(
            x2, vec(g_attn_pre), mod3, tabs, _split_w_in(w_in[l]), seq)
        lams = [vec(lambda_q1), vec(lambda_k1), vec(lambda_q2), vec(lambda_k2)]
        diff_out = _diff_attention(odq, odk, odv, lams, vec(g_diff_sub), bsz, seq, lam_init)
        dsa_out = _dsa_attention(oiq, oiw, oik, osq, osk, osv, bsz, seq, k_top)
        x1, h2 = _out_proj(diff_out, dsa_out, w_o[l].astype(BF16), x2, mod3,
                           vec(g_attn_post), vec(g_ffn_pre), seq)
        x2 = _ffn(h2, _col_tiles(w_gate[l]), _col_tiles(w_up[l]), w_down[l].astype(BF16),
                  x1, mod3, vec(g_ffn_post), seq)
    return x2.reshape(bsz, seq, d)
```
